```python
import math
import jax, jax.numpy as jnp
from jax import lax
import numpy as np

D_MODEL = 4096
BATCH = 4
SEQ = 2048
DEPTH = 2
DEC_BATCH = 8
DEC_SEQ = 8
PAST_LEN = 16384
PAGE_SIZE = 128

HEAD_DIM = 128
POOL_WIDTH = D_MODEL // 4
POOL_WINDOWS = (2, 4, 8, 16)
POOL_GROUP = POOL_WIDTH // len(POOL_WINDOWS)
POOL_CARRY = max(POOL_WINDOWS) - 1
DIFF_HEADS = (D_MODEL - POOL_WIDTH) // (2 * HEAD_DIM)
DIFF_WIDTH = DIFF_HEADS * 2 * HEAD_DIM
EVEN_IN = POOL_WIDTH + 3 * DIFF_WIDTH
EVEN_MIX = POOL_WIDTH + DIFF_WIDTH
MOBA_HEADS = D_MODEL // HEAD_DIM
MOBA_WIDTH = MOBA_HEADS * HEAD_DIM
MOBA_BLOCK = 256
MOBA_TOPK = 3
MOBA_QCHUNK = 8
ATTN_QBLOCK = 128
MOE_GROUPS = 4
MOE_PER_GROUP = 8
N_EXPERTS = MOE_GROUPS * MOE_PER_GROUP
MOE_TOPK = 2
D_EXPERT = D_MODEL // 8
N_EVEN = (DEPTH + 1) // 2
N_ODD = DEPTH // 2
ALPHA = (2.0 * DEPTH) ** 0.25
BETA = (8.0 * DEPTH) ** -0.25
LN_EPS = 1e-5
RMS_EPS = 1e-5

kernel_name = 'hybrid_pool_diffattn_moba_hmoe_step'

F32 = jnp.float32


def alibi_slopes(n):
    def pow2(m):
        start = 2.0 ** (-(2.0 ** -(math.log2(m) - 3)))
        return [start ** (i + 1) for i in range(m)]
    if (n & (n - 1)) == 0:
        s = pow2(n)
    else:
        c = 2 ** int(math.floor(math.log2(n)))
        s = pow2(c) + pow2(2 * c)[0::2][: n - c]
    return jnp.asarray(np.asarray(s, np.float32))


def layer_norm(x, g, b):
    xf = x.astype(F32)
    mu = jnp.mean(xf, -1, keepdims=True)
    xc = xf - mu
    var = jnp.mean(xc * xc, -1, keepdims=True)
    return (xc * lax.rsqrt(var + LN_EPS) * g.astype(F32) + b.astype(F32)).astype(x.dtype)


def split_even(h):
    bsz, length, _ = h.shape
    u = h[..., :POOL_WIDTH]
    q = h[..., POOL_WIDTH:POOL_WIDTH + DIFF_WIDTH].reshape(bsz, length, DIFF_HEADS, 2, HEAD_DIM)
    k = h[..., POOL_WIDTH + DIFF_WIDTH:POOL_WIDTH + 2 * DIFF_WIDTH].reshape(bsz, length, DIFF_HEADS, 2, HEAD_DIM)
    v = h[..., POOL_WIDTH + 2 * DIFF_WIDTH:].reshape(bsz, length, DIFF_HEADS, 2 * HEAD_DIM)
    return u, q, k, v


def split_odd(h):
    bsz, length, _ = h.shape
    q = h[..., :MOBA_WIDTH].reshape(bsz, length, MOBA_HEADS, HEAD_DIM)
    k = h[..., MOBA_WIDTH:2 * MOBA_WIDTH].reshape(bsz, length, MOBA_HEADS, HEAD_DIM)
    v = h[..., 2 * MOBA_WIDTH:].reshape(bsz, length, MOBA_HEADS, HEAD_DIM)
    return q, k, v


def heads_to_tokens(o):
    bsz, nh, length, dh = o.shape
    return jnp.transpose(o, (0, 2, 1, 3)).reshape(bsz, length, nh * dh)


def pool_mix(prefix, u, pos, pool_w, pool_scale):
    bsz, length, _ = u.shape
    ext = jnp.concatenate([prefix.astype(F32), u.astype(F32)], axis=1)
    cs = jnp.concatenate([jnp.zeros((bsz, 1, POOL_WIDTH), F32), jnp.cumsum(ext, axis=1)], axis=1)
    cur = ext[:, POOL_CARRY:]
    end = cs[:, POOL_CARRY + 1:]
    groups = []
    for g, w in enumerate(POOL_WINDOWS):
        c0, c1 = g * POOL_GROUP, (g + 1) * POOL_GROUP
        win = end[..., c0:c1] - cs[:, POOL_CARRY + 1 - w:POOL_CARRY + 1 - w + length, c0:c1]
        cnt = jnp.minimum(w, pos + 1).astype(F32)[None, :, None]
        groups.append(win / cnt - cur[..., c0:c1])
    pooled = jnp.stack(groups, axis=2)
    y = jnp.einsum('blgc,gcd->blgd', pooled, pool_w.astype(F32)).reshape(bsz, length, POOL_WIDTH)
    y = y * pool_scale.astype(F32)
    return y.astype(u.dtype), ext[:, -POOL_CARRY:].astype(u.dtype)


def diff_combine(o, lam, subln_g, lam_init):
    lam = lam.astype(F32)
    lam_full = jnp.exp(jnp.sum(lam[0] * lam[1])) - jnp.exp(jnp.sum(lam[2] * lam[3])) + lam_init
    d = o[:, :, 0] - lam_full * o[:, :, 1]
    d = d * lax.rsqrt(jnp.mean(d * d, -1, keepdims=True) + RMS_EPS) * subln_g.astype(F32) * (1.0 - lam_init)
    return heads_to_tokens(d)


def diff_attn_prompt(q, k, v, slopes):
    bsz, s = q.shape[:2]
    scale = HEAD_DIM ** -0.5
    qh = jnp.transpose(q, (0, 2, 3, 1, 4)).astype(F32) * scale
    kh = jnp.transpose(k, (0, 2, 3, 1, 4)).astype(F32)
    vf = v.astype(F32)
    kpos = jnp.arange(s)

    def qblock(start):
        qb = lax.dynamic_slice_in_dim(qh, start, ATTN_QBLOCK, axis=3)
        qpos = start + jnp.arange(ATTN_QBLOCK)
        dist = (qpos[:, None] - kpos[None, :]).astype(F32)
        sc = jnp.einsum('bhmqd,bhmkd->bhmqk', qb, kh) - slopes[:, None, None, None] * dist
        sc = jnp.where(dist >= 0, sc, -jnp.inf)
        p = jax.nn.softmax(sc, axis=-1)
        return jnp.einsum('bhmqk,bkhd->bhmqd', p, vf)

    o = lax.map(qblock, jnp.arange(0, s, ATTN_QBLOCK))
    return jnp.moveaxis(o, 0, 3).reshape(bsz, DIFF_HEADS, 2, s, 2 * HEAD_DIM)


def _online(carry, sc, v):
    m, l, acc = carry
    m_new = jnp.maximum(m, jnp.max(sc, -1))
    corr = jnp.exp(m - m_new)
    p = jnp.exp(sc - m_new[..., None])
    l = l * corr + jnp.sum(p, -1)
    acc = acc * corr[..., None] + jnp.einsum('bhmqk,bkhd->bhmqd', p, v.astype(F32))
    return m_new, l, acc


def diff_attn_sample(q, k, v, cache_k, cache_v, li, page_table, slopes):
    db, ds = q.shape[:2]
    n_pages = page_table.shape[1]
    past = n_pages * PAGE_SIZE
    scale = HEAD_DIM ** -0.5
    qh = jnp.transpose(q, (0, 2, 3, 1, 4)).astype(F32) * scale
    qpos = past + jnp.arange(ds)
    m0 = jnp.full((db, DIFF_HEADS, 2, ds), -jnp.inf, F32)
    l0 = jnp.zeros((db, DIFF_HEADS, 2, ds), F32)
    a0 = jnp.zeros((db, DIFF_HEADS, 2, ds, 2 * HEAD_DIM), F32)

    def page_step(carry, xs):
        pno, pages = xs
        kp = cache_k[li, pages].reshape(db, PAGE_SIZE, DIFF_HEADS, 2, HEAD_DIM).astype(F32)
        vp = cache_v[li, pages]
        dist = (qpos[:, None] - (pno * PAGE_SIZE + jnp.arange(PAGE_SIZE))[None, :]).astype(F32)
        sc = jnp.einsum('bhmqd,bkhmd->bhmqk', qh, kp) - slopes[:, None, None, None] * dist
        return _online(carry, sc, vp), None

    carry, _ = lax.scan(page_step, (m0, l0, a0), (jnp.arange(n_pages), page_table.T))
    dist = (qpos[:, None] - qpos[None, :]).astype(F32)
    sc = jnp.einsum('bhmqd,bkhmd->bhmqk', qh, k.astype(F32)) - slopes[:, None, None, None] * dist
    sc = jnp.where(dist >= 0, sc, -jnp.inf)
    _, l, acc = _online(carry, sc, v)
    return acc / l[..., None]


def _moba_mix(s_sel, v_sel, s_own, v_own):
    p = jax.nn.softmax(jnp.concatenate([s_sel, s_own], axis=-1), axis=-1)
    n = s_sel.shape[-1]
    return (jnp.einsum('bhqn,bhqnd->bhqd', p[..., :n], v_sel)
            + jnp.einsum('bhql,bhld->bhqd', p[..., n:], v_own))


def moba_prompt(q, k, v, slopes):
    bsz, s = q.shape[:2]
    scale = HEAD_DIM ** -0.5
    nb = -(-s // MOBA_BLOCK)
    pad = nb * MOBA_BLOCK - s

    def blocks(t):
        t = jnp.pad(t.astype(F32), ((0, 0), (0, pad), (0, 0), (0, 0)))
        return jnp.transpose(t.reshape(bsz, nb, MOBA_BLOCK, MOBA_HEADS, HEAD_DIM), (0, 3, 1, 2, 4))

    kb, vb = blocks(k), blocks(v)
    kmean = jnp.mean(kb, axis=3)
    qh = jnp.transpose(q, (0, 2, 1, 3)).astype(F32) * scale
    ksel = min(MOBA_TOPK, nb - 1)
    bi = jnp.arange(bsz)[:, None, None, None]
    hi = jnp.arange(MOBA_HEADS)[None, :, None, None]
    sl = slopes[:, None, None]

    def chunk(start):
        qc = lax.dynamic_slice_in_dim(qh, start, MOBA_QCHUNK, axis=2)
        qpos = start + jnp.arange(MOBA_QCHUNK)
        ob = start // MOBA_BLOCK
        k_own = lax.dynamic_index_in_dim(kb, ob, axis=2, keepdims=False)
        v_own = lax.dynamic_index_in_dim(vb, ob, axis=2, keepdims=False)
        pos_own = ob * MOBA_BLOCK + jnp.arange(MOBA_BLOCK)
        dist = (qpos[:, None] - pos_own[None, :]).astype(F32)
        s_own = jnp.einsum('bhqd,bhkd->bhqk', qc, k_own) - sl * dist
        s_own = jnp.where(dist >= 0, s_own, -jnp.inf)
        if ksel == 0:
            return jnp.einsum('bhqk,bhkd->bhqd', jax.nn.softmax(s_own, axis=-1), v_own)
        gate = jnp.einsum('bhqd,bhnd->bhqn', qc, kmean)
        gate = jnp.where(jnp.arange(nb) < ob, gate, -jnp.inf)
        _, sel = lax.top_k(gate, ksel)
        n_sel = ksel * MOBA_BLOCK
        k_sel = kb[bi, hi, sel].reshape(bsz, MOBA_HEADS, MOBA_QCHUNK, n_sel, HEAD_DIM)
        v_sel = vb[bi, hi, sel].reshape(bsz, MOBA_HEADS, MOBA_QCHUNK, n_sel, HEAD_DIM)
        pos_sel = (sel[..., None] * MOBA_BLOCK + jnp.arange(MOBA_BLOCK)).reshape(bsz, MOBA_HEADS, MOBA_QCHUNK, n_sel)
        valid = jnp.repeat(jnp.arange(ksel) < ob, MOBA_BLOCK)
        dist_sel = (qpos[:, None] - pos_sel).astype(F32)
        s_sel = jnp.einsum('bhqd,bhqnd->bhqn', qc, k_sel) - sl * dist_sel
        s_sel = jnp.where(valid, s_sel, -jnp.inf)
        return _moba_mix(s_sel, v_sel, s_own, v_own)

    o = lax.map(chunk, jnp.arange(0, s, MOBA_QCHUNK))
    return jnp.moveaxis(o, 0, 2).reshape(bsz, MOBA_HEADS, s, HEAD_DIM)


def moba_sample(q, k, v, cache_k, cache_v, li, page_table, slopes):
    db, ds = q.shape[:2]
    n_pages = page_table.shape[1]
    past = n_pages * PAGE_SIZE
    bp = MOBA_BLOCK // PAGE_SIZE
    n_full = past // MOBA_BLOCK
    own_rows = past - n_full * MOBA_BLOCK
    scale = HEAD_DIM ** -0.5
    qh = jnp.transpose(q, (0, 2, 1, 3)).astype(F32) * scale
    qpos = past + jnp.arange(ds)
    sl = slopes[:, None, None]
    own_pages = page_table[:, n_full * bp:]
    k_own = jnp.concatenate([cache_k[li, own_pages].reshape(db, own_rows, MOBA_HEADS, HEAD_DIM).astype(F32),
                             k.astype(F32)], axis=1)
    v_own = jnp.concatenate([cache_v[li, own_pages].reshape(db, own_rows, MOBA_HEADS, HEAD_DIM).astype(F32),
                             v.astype(F32)], axis=1)
    k_own = jnp.transpose(k_own, (0, 2, 1, 3))
    v_own = jnp.transpose(v_own, (0, 2, 1, 3))
    pos_own = n_full * MOBA_BLOCK + jnp.arange(own_rows + ds)
    dist = (qpos[:, None] - pos_own[None, :]).astype(F32)
    s_own = jnp.einsum('bhqd,bhkd->bhqk', qh, k_own) - sl * dist
    s_own = jnp.where(dist >= 0, s_own, -jnp.inf)
    ksel = min(MOBA_TOPK, n_full)
    if ksel == 0:
        return jnp.einsum('bhqk,bhkd->bhqd', jax.nn.softmax(s_own, axis=-1), v_own)

    def block_mean(blk):
        pg = lax.dynamic_slice_in_dim(page_table, blk * bp, bp, axis=1)
        return jnp.mean(cache_k[li, pg].astype(F32), axis=(1, 2))

    kmean = lax.map(block_mean, jnp.arange(n_full))
    gate = jnp.einsum('bhqd,nbhd->bhqn', qh, kmean)
    _, sel = lax.top_k(gate, ksel)
    bi = jnp.arange(db)[:, None, None, None, None]
    pid = page_table[bi, sel[..., None] * bp + jnp.arange(bp)]
    hi = jnp.arange(MOBA_HEADS)[None, :, None, None, None, None]
    rows = jnp.arange(PAGE_SIZE)
    n_sel = ksel * MOBA_BLOCK
    k_sel = cache_k[li, pid[..., None], rows, hi].astype(F32).reshape(db, MOBA_HEADS, ds, n_sel, HEAD_DIM)
    v_sel = cache_v[li, pid[..., None], rows, hi].astype(F32).reshape(db, MOBA_HEADS, ds, n_sel, HEAD_DIM)
    pos_sel = (sel[..., None] * MOBA_BLOCK + jnp.arange(MOBA_BLOCK)).reshape(db, MOBA_HEADS, ds, n_sel)
    dist_sel = (qpos[:, None] - pos_sel).astype(F32)
    s_sel = jnp.einsum('bhqd,bhqnd->bhqn', qh, k_sel) - sl * dist_sel
    return _moba_mix(s_sel, v_sel, s_own, v_own)


def moe_dispatch(xt, eid, wgt, li, w1, w3, w2):
    n_tok, d = xt.shape
    n_asg = n_tok * MOE_TOPK
    blk = int(min(128, max(8, n_asg // N_EXPERTS)))
    n_blk = -(-(n_asg + N_EXPERTS * (blk - 1)) // blk)
    e_flat = eid.reshape(n_asg)
    order = jnp.argsort(e_flat)
    e_sorted = e_flat[order]
    tok_sorted = (order // MOE_TOPK).astype(jnp.int32)
    counts = jnp.bincount(e_flat, length=N_EXPERTS)
    starts = jnp.cumsum(counts) - counts
    padded = (counts + blk - 1) // blk * blk
    pad_ends = jnp.cumsum(padded)
    dest = pad_ends[e_sorted] - padded[e_sorted] + jnp.arange(n_asg) - starts[e_sorted]
    slot_tok = jnp.full((n_blk * blk,), n_tok, jnp.int32).at[dest].set(tok_sorted)
    xpad = jnp.concatenate([xt, jnp.zeros((1, d), xt.dtype)], axis=0)
    xb = xpad[slot_tok].reshape(n_blk, blk, d)
    blk_e = jnp.minimum(jnp.searchsorted(pad_ends, jnp.arange(n_blk) * blk, side='right'), N_EXPERTS - 1)

    def run(args):
        xblk, e = args
        h = jax.nn.silu(xblk @ w1[li, e]) * (xblk @ w3[li, e])
        return h @ w2[li, e]

    yb = lax.map(run, (xb, blk_e)).reshape(n_blk * blk, d)
    contrib = yb[dest].astype(F32) * wgt.reshape(n_asg)[order][:, None]
    return jax.ops.segment_sum(contrib, tok_sorted, num_segments=n_tok).astype(xt.dtype)


def hier_moe(xt, li, rg_w, rg_b, re_w, re_b, w1, w3, w2):
    n_tok = xt.shape[0]
    xf = xt.astype(F32)
    lg = xf @ rg_w[li].astype(F32) + rg_b[li].astype(F32)
    pg = jax.nn.softmax(lg, axis=-1)
    g_top = jnp.argmax(lg, axis=-1).astype(jnp.int32)
    gate_g = jnp.take_along_axis(pg, g_top[:, None], axis=-1)
    le = (xf @ re_w[li].astype(F32) + re_b[li].astype(F32)).reshape(n_tok, MOE_GROUPS, MOE_PER_GROUP)
    le = jnp.take_along_axis(le, g_top[:, None, None], axis=1)[:, 0]
    top_v, top_i = lax.top_k(le, MOE_TOPK)
    wgt = gate_g * jax.nn.softmax(top_v, axis=-1)
    eid = g_top[:, None] * MOE_PER_GROUP + top_i.astype(jnp.int32)
    return moe_dispatch(xt, eid, wgt, li, w1, w3, w2)


def setup_inputs(seed: int = 0) -> dict:
    key = jax.random.key(seed)
    ks = jax.random.split(key, 26)
    n_pages = PAST_LEN // PAGE_SIZE
    n_used = DEC_BATCH * n_pages
    n_phys = n_used + max(1, n_used // 4)

    def nrm(k, shape, scale=1.0):
        return scale * jax.random.normal(k, shape, F32)

    page_table = jax.random.permutation(ks[7], n_phys)[:n_used].reshape(DEC_BATCH, n_pages).astype(jnp.int32)
    return {
        'x_prompt': nrm(ks[0], (BATCH, SEQ, D_MODEL)),
        'x_sample': nrm(ks[1], (DEC_BATCH, DEC_SEQ, D_MODEL)),
        'cache_k_diff': nrm(ks[2], (N_EVEN, n_phys, PAGE_SIZE, DIFF_HEADS, 2 * HEAD_DIM)),
        'cache_v_diff': nrm(ks[3], (N_EVEN, n_phys, PAGE_SIZE, DIFF_HEADS, 2 * HEAD_DIM)),
        'state_pool': nrm(ks[4], (N_EVEN, DEC_BATCH, POOL_CARRY, POOL_WIDTH)),
        'cache_k_moba': nrm(ks[5], (N_ODD, n_phys, PAGE_SIZE, MOBA_HEADS, HEAD_DIM)),
        'cache_v_moba': nrm(ks[6], (N_ODD, n_phys, PAGE_SIZE, MOBA_HEADS, HEAD_DIM)),
        'page_table': page_table,
        'w_in_even': nrm(ks[8], (N_EVEN, D_MODEL, EVEN_IN), D_MODEL ** -0.5),
        'pool_w': nrm(ks[9], (N_EVEN, len(POOL_WINDOWS), POOL_GROUP, POOL_GROUP), POOL_GROUP ** -0.5),
        'pool_scale': 1.0 + nrm(ks[10], (N_EVEN, POOL_WIDTH), 0.1),
        'diff_lambda': nrm(ks[11], (N_EVEN, 4, HEAD_DIM), 0.1),
        'diff_subln_g': 1.0 + nrm(ks[12], (N_EVEN, 2 * HEAD_DIM), 0.1),
        'w_out_even': nrm(ks[13], (N_EVEN, EVEN_MIX, D_MODEL), BETA * EVEN_MIX ** -0.5),
        'w_in_odd': nrm(ks[14], (N_ODD, D_MODEL, 3 * MOBA_WIDTH), D_MODEL ** -0.5),
        'w_out_odd': nrm(ks[15], (N_ODD, MOBA_WIDTH, D_MODEL), BETA * MOBA_WIDTH ** -0.5),
        'ln_g': 1.0 + nrm(ks[16], (DEPTH, 2, D_MODEL), 0.05),
        'ln_b': nrm(ks[17], (DEPTH, 2, D_MODEL), 0.05),
        'router_g_w': nrm(ks[18], (DEPTH, D_MODEL, MOE_GROUPS), D_MODEL ** -0.5),
        'router_g_b': nrm(ks[19], (DEPTH, MOE_GROUPS), 0.01),
        'router_e_w': nrm(ks[20], (DEPTH, D_MODEL, N_EXPERTS), D_MODEL ** -0.5),
        'router_e_b': nrm(ks[21], (DEPTH, N_EXPERTS), 0.01),
        'moe_w1': nrm(ks[22], (DEPTH, N_EXPERTS, D_MODEL, D_EXPERT), D_MODEL ** -0.5),
        'moe_w3': nrm(ks[23], (DEPTH, N_EXPERTS, D_MODEL, D_EXPERT), D_MODEL ** -0.5),
        'moe_w2': nrm(ks[24], (DEPTH, N_EXPERTS, D_EXPERT, D_MODEL), BETA * D_EXPERT ** -0.5),
    }


def reference(x_prompt, x_sample, cache_k_diff, cache_v_diff, state_pool, cache_k_moba, cache_v_moba,
              page_table, w_in_even, pool_w, pool_scale, diff_lambda, diff_subln_g, w_out_even,
              w_in_odd, w_out_odd, ln_g, ln_b, router_g_w, router_g_b, router_e_w, router_e_b,
              moe_w1, moe_w3, moe_w2):
    slopes_b = alibi_slopes(DIFF_HEADS)
    slopes_c = alibi_slopes(MOBA_HEADS)
    b, s, d = x_prompt.shape
    db, ds, _ = x_sample.shape
    past = page_table.shape[1] * PAGE_SIZE
    xp, xs = x_prompt, x_sample
    kd_p, vd_p, pl_p, km_p, vm_p = [], [], [], [], []
    kd_s, vd_s, pl_s, km_s, vm_s = [], [], [], [], []
    for i in range(DEPTH):
        j = i // 2
        if i % 2 == 0:
            lam_init = 0.8 - 0.6 * math.exp(-0.3 * i)
            up, qp, kp, vp = split_even(xp @ w_in_even[j])
            pool_p, carry_p = pool_mix(jnp.zeros((b, POOL_CARRY, POOL_WIDTH), up.dtype), up, jnp.arange(s),
                                       pool_w[j], pool_scale[j])
            att_p = diff_combine(diff_attn_prompt(qp, kp, vp, slopes_b), diff_lambda[j], diff_subln_g[j], lam_init)
            mix_p = jnp.concatenate([pool_p, att_p.astype(xp.dtype)], axis=-1) @ w_out_even[j]
            us, qs, ks_, vs = split_even(xs @ w_in_even[j])
            pool_s, carry_s = pool_mix(state_pool[j], us, past + jnp.arange(ds), pool_w[j], pool_scale[j])
            att_s = diff_combine(diff_attn_sample(qs, ks_, vs, cache_k_diff, cache_v_diff, j, page_table, slopes_b),
                                 diff_lambda[j], diff_subln_g[j], lam_init)
            mix_s = jnp.concatenate([pool_s, att_s.astype(xs.dtype)], axis=-1) @ w_out_even[j]
            kd_p.append(kp.reshape(b, s, DIFF_HEADS, 2 * HEAD_DIM))
            vd_p.append(vp)
            pl_p.append(carry_p)
            kd_s.append(ks_.reshape(db, ds, DIFF_HEADS, 2 * HEAD_DIM))
            vd_s.append(vs)
            pl_s.append(carry_s)
        else:
            qp, kp, vp = split_odd(xp @ w_in_odd[j])
            mix_p = heads_to_tokens(moba_prompt(qp, kp, vp, slopes_c)).astype(xp.dtype) @ w_out_odd[j]
            qs, ks_, vs = split_odd(xs @ w_in_odd[j])
            mix_s = heads_to_tokens(moba_sample(qs, ks_, vs, cache_k_moba, cache_v_moba, j, page_table,
                                                slopes_c)).astype(xs.dtype) @ w_out_odd[j]
            km_p.append(kp)
            vm_p.append(vp)
            km_s.append(ks_)
            vm_s.append(vs)
        xp = layer_norm(ALPHA * xp + mix_p, ln_g[i, 0], ln_b[i, 0])
        xs = layer_norm(ALPHA * xs + mix_s, ln_g[i, 0], ln_b[i, 0])
        xt = jnp.concatenate([xp.reshape(b * s, d), xs.reshape(db * ds, d)], axis=0)
        ff = hier_moe(xt, i, router_g_w, router_g_b, router_e_w, router_e_b, moe_w1, moe_w3, moe_w2)
        xt = layer_norm(ALPHA * xt + ff, ln_g[i, 1], ln_b[i, 1])
        xp = xt[:b * s].reshape(b, s, d)
        xs = xt[b * s:].reshape(db, ds, d)
    return (xp, xs, jnp.stack(kd_p), jnp.stack(vd_p), jnp.stack(pl_p), jnp.stack(km_p), jnp.stack(vm_p),
            jnp.stack(kd_s), jnp.stack(vd_s), jnp.stack(pl_s), jnp.stack(km_s), jnp.stack(vm_s))
```

```python
import functools
import math

import numpy as np
import jax
import jax.numpy as jnp
from jax import lax
from jax.experimental import pallas as pl
from jax.experimental.pallas import tpu as pltpu

F32 = jnp.float32
BF16 = jnp.bfloat16
I32 = jnp.int32

HEAD_DIM = 128
POOL_WINDOWS = (2, 4, 8, 16)
POOL_CARRY = max(POOL_WINDOWS) - 1
POOL_HALO = 16
MOBA_BLOCK = 256
MOBA_TOPK = 3
PAGE_SIZE = 128
MOE_GROUPS = 4
MOE_PER_GROUP = 8
N_EXPERTS = MOE_GROUPS * MOE_PER_GROUP
MOE_TOPK = 2
LN_EPS = 1e-5
RMS_EPS = 1e-5
NEG_INF = float("-inf")

MOE_TILE = 256
ROUTER_LANES = 128
VMEM_LIMIT = 56 * 1024 * 1024


def _alibi_slopes(n):
    def pow2(m):
        start = 2.0 ** (-(2.0 ** -(math.log2(m) - 3)))
        return [start ** (i + 1) for i in range(m)]
    if (n & (n - 1)) == 0:
        s = pow2(n)
    else:
        c = 2 ** int(math.floor(math.log2(n)))
        s = pow2(c) + pow2(2 * c)[0::2][: n - c]
    return np.asarray(s, np.float32)


def _row_tile(n, cap):
    best = max(t for t in range(16, cap + 1, 16) if n % t == 0)
    return best


def _params(semantics):
    return pltpu.CompilerParams(dimension_semantics=semantics, vmem_limit_bytes=VMEM_LIMIT)


def _dot_t(a, b, precision=None):
    return lax.dot_general(a, b, (((1,), (1,)), ((), ())), preferred_element_type=F32,
                           precision=precision)


def _mm_kernel(*refs, k_splits, n_out, has_resid, alpha, scales, cast_rows):
    n_x = len(k_splits)
    x_refs = refs[:n_x]
    w_ref = refs[n_x]
    pos = n_x + 1
    resid_ref = refs[pos] if has_resid else None
    pos += int(has_resid)
    out_refs = refs[pos:pos + n_out]
    wbf_ref = refs[pos + n_out]

    @pl.when(pl.program_id(1) == 0)
    def _():
        def body(r, c):
            rows = pl.ds(pl.multiple_of(r * cast_rows, cast_rows), cast_rows)
            wbf_ref[rows, :] = w_ref[rows, :].astype(BF16)
            return c
        lax.fori_loop(0, w_ref.shape[0] // cast_rows, body, 0)

    acc = None
    for x_ref, (k0, k1) in zip(x_refs, k_splits):
        part = jnp.dot(x_ref[...], wbf_ref[k0:k1, :], preferred_element_type=F32)
        acc = part if acc is None else acc + part
    if has_resid:
        acc = alpha * resid_ref[...] + acc
    for o_ref, sc in zip(out_refs, scales):
        o_ref[...] = (acc if sc == 1.0 else acc * sc).astype(o_ref.dtype)


def _matmul(xs, w, layer, col0, ncols, outs, *, tm, tn=512, resid=None, alpha=1.0):
    m = xs[0].shape[0]
    k_total = w.shape[1]
    tm = min(tm, m)
    assert m % tm == 0 and ncols % tn == 0 and col0 % tn == 0
    k_splits, k0 = [], 0
    for x in xs:
        k_splits.append((k0, k0 + x.shape[1]))
        k0 += x.shape[1]
    assert k0 == k_total
    cb = col0 // tn
    in_specs = [pl.BlockSpec((tm, x.shape[1]), lambda j, i: (i, 0)) for x in xs]
    in_specs.append(pl.BlockSpec((None, k_total, tn), lambda j, i: (layer, 0, cb + j)))
    args = list(xs) + [w]
    if resid is not None:
        in_specs.append(pl.BlockSpec((tm, tn), lambda j, i: (i, j)))
        args.append(resid)
    kern = functools.partial(
        _mm_kernel, k_splits=tuple(k_splits), n_out=len(outs), has_resid=resid is not None,
        alpha=alpha, scales=tuple(s for _, s in outs), cast_rows=min(512, k_total))
    res = pl.pallas_call(
        kern,
        grid=(ncols // tn, m // tm),
        in_specs=in_specs,
        out_specs=[pl.BlockSpec((tm, tn), lambda j, i: (i, j)) for _ in outs],
        out_shape=[jax.ShapeDtypeStruct((m, ncols), dt) for dt, _ in outs],
        scratch_shapes=[pltpu.VMEM((k_total, tn), BF16)],
        compiler_params=_params(("arbitrary", "arbitrary")),
    )(*args)
    return res


def _pool_kernel(u_ref, pre_ref, w_ref, sc_ref, o_ref, ext_ref, *, tl, pos0, group):
    li = pl.program_id(1)

    @pl.when(li == 0)
    def _():
        ext_ref[0:POOL_HALO, :] = pre_ref[...]

    ext_ref[POOL_HALO:POOL_HALO + tl, :] = u_ref[...]
    pos = pos0 + li * tl + lax.broadcasted_iota(I32, (tl, 1), 0)
    for g, win in enumerate(POOL_WINDOWS):
        c0, c1 = g * group, (g + 1) * group
        cur = ext_ref[POOL_HALO:POOL_HALO + tl, c0:c1]
        tot = cur
        for i in range(1, win):
            tot = tot + ext_ref[POOL_HALO - i:POOL_HALO - i + tl, c0:c1]
        cnt = jnp.minimum(win, pos + 1).astype(F32)
        pooled = tot / cnt - cur
        y = jnp.dot(pooled.astype(BF16), w_ref[g].astype(BF16), preferred_element_type=F32)
        o_ref[:, c0:c1] = (y * sc_ref[:, c0:c1]).astype(o_ref.dtype)
    if tl >= POOL_HALO:
        ext_ref[0:POOL_HALO, :] = ext_ref[tl:tl + POOL_HALO, :]


def _pool_mix(u, prefix, pool_w, pool_scale, layer, pos0, *, tl):
    bsz, length, width = u.shape
    tl = min(tl, length)
    assert length % tl == 0 and (length == tl or tl >= POOL_HALO)
    group = width // len(POOL_WINDOWS)
    kern = functools.partial(_pool_kernel, tl=tl, pos0=pos0, group=group)
    out = pl.pallas_call(
        kern,
        grid=(bsz, length // tl),
        in_specs=[
            pl.BlockSpec((None, tl, width), lambda b, l: (b, l, 0)),
            pl.BlockSpec((None, POOL_HALO, width), lambda b, l: (b, 0, 0)),
            pl.BlockSpec((None, len(POOL_WINDOWS), group, group), lambda b, l: (layer, 0, 0, 0)),
            pl.BlockSpec((None, 1, width), lambda b, l: (layer, 0, 0)),
        ],
        out_specs=pl.BlockSpec((None, tl, width), lambda b, l: (b, l, 0)),
        out_shape=jax.ShapeDtypeStruct((bsz, length, width), BF16),
        scratch_shapes=[pltpu.VMEM((POOL_HALO + tl, width), F32)],
        compiler_params=_params(("arbitrary", "arbitrary")),
    )(u, prefix, pool_w, pool_scale.reshape(pool_scale.shape[0], 1, width))
    return out.reshape(bsz * length, width)


def _online_update(m_ref, l_ref, acc_ref, idx, s, pv_fn):
    m_prev = m_ref[idx]
    m_new = jnp.maximum(m_prev, jnp.max(s, axis=-1, keepdims=True))
    corr = jnp.exp(m_prev - m_new)
    p = jnp.exp(s - m_new)
    l_ref[idx] = corr * l_ref[idx] + jnp.sum(p, axis=-1, keepdims=True)
    acc_ref[idx] = corr * acc_ref[idx] + pv_fn(p)
    m_ref[idx] = m_new


def _diff_combine(o1, o2, lam_ref, g_ref, lam_init):
    lam = lam_ref[...]
    e1 = jnp.exp(jnp.sum(lam[0:1] * lam[1:2], axis=-1, keepdims=True))
    e2 = jnp.exp(jnp.sum(lam[2:3] * lam[3:4], axis=-1, keepdims=True))
    lam_full = e1 - e2 + lam_init
    d = o1 - lam_full * o2
    ms = jnp.mean(d * d, axis=-1, keepdims=True)
    return d * lax.rsqrt(ms + RMS_EPS) * g_ref[...] * (1.0 - lam_init)


def _diff_prompt_kernel(slopes_ref, q1_ref, q2_ref, k1_ref, k2_ref, v_ref, lam_ref, g_ref, o_ref,
                        m_ref, l_ref, acc_ref, *, tq, lam_init):
    h = pl.program_id(1)
    qi = pl.program_id(2)
    slope = slopes_ref[h]
    row = lax.broadcasted_iota(I32, (tq, tq), 0)
    col = lax.broadcasted_iota(I32, (tq, tq), 1)
    rel = (col - row).astype(F32) * slope
    qs = (q1_ref[...], q2_ref[...])
    k_refs = (k1_ref, k2_ref)

    d0 = pl.multiple_of(qi * tq, tq)
    v_d = v_ref[pl.ds(d0, tq), :]
    for mi in range(2):
        s = _dot_t(qs[mi], k_refs[mi][pl.ds(d0, tq), :]) + rel
        s = jnp.where(col <= row, s, NEG_INF)
        m0 = jnp.max(s, axis=-1, keepdims=True)
        p = jnp.exp(s - m0)
        m_ref[mi] = m0
        l_ref[mi] = jnp.sum(p, axis=-1, keepdims=True)
        acc_ref[mi] = jnp.dot(p.astype(BF16), v_d, preferred_element_type=F32)

    def body(kj, c):
        k0 = pl.multiple_of(kj * tq, tq)
        v_t = v_ref[pl.ds(k0, tq), :]
        bias = rel - slope * ((qi - kj) * tq).astype(F32)
        for mi in range(2):
            s = _dot_t(qs[mi], k_refs[mi][pl.ds(k0, tq), :]) + bias
            _online_update(m_ref, l_ref, acc_ref, mi, s,
                           lambda p: jnp.dot(p.astype(BF16), v_t, preferred_element_type=F32))
        return c

    lax.fori_loop(0, qi, body, 0)
    o1 = acc_ref[0] / l_ref[0]
    o2 = acc_ref[1] / l_ref[1]
    o_ref[...] = _diff_combine(o1, o2, lam_ref, g_ref, lam_init).astype(o_ref.dtype)


def _diff_attn_prompt(q, k, v, slopes, diff_lambda, subln_g, layer, bsz, seq, lam_init, *, tq=256):
    dv = 2 * HEAD_DIM
    n_heads = q.shape[1] // dv
    tq = min(tq, seq)
    assert seq % tq == 0
    nq = seq // tq
    kern = functools.partial(_diff_prompt_kernel, tq=tq, lam_init=lam_init)
    return pl.pallas_call(
        kern,
        grid=(bsz, n_heads, nq),
        in_specs=[
            pl.BlockSpec(memory_space=pltpu.SMEM),
            pl.BlockSpec((tq, HEAD_DIM), lambda b, h, i: (b * nq + i, 2 * h)),
            pl.BlockSpec((tq, HEAD_DIM), lambda b, h, i: (b * nq + i, 2 * h + 1)),
            pl.BlockSpec((seq, HEAD_DIM), lambda b, h, i: (b, 2 * h)),
            pl.BlockSpec((seq, HEAD_DIM), lambda b, h, i: (b, 2 * h + 1)),
            pl.BlockSpec((seq, dv), lambda b, h, i: (b, h)),
            pl.BlockSpec((None, 4, HEAD_DIM), lambda b, h, i: (layer, 0, 0)),
            pl.BlockSpec((None, 1, dv), lambda b, h, i: (layer, 0, 0)),
        ],
        out_specs=pl.BlockSpec((tq, dv), lambda b, h, i: (b * nq + i, h)),
        out_shape=jax.ShapeDtypeStruct(q.shape, BF16),
        scratch_shapes=[pltpu.VMEM((2, tq, 1), F32), pltpu.VMEM((2, tq, 1), F32),
                        pltpu.VMEM((2, tq, dv), F32)],
        compiler_params=_params(("arbitrary", "arbitrary", "arbitrary")),
    )(slopes, q, q, k, k, v, diff_lambda, subln_g.reshape(subln_g.shape[0], 1, dv))


def _diff_sample_kernel(pt_ref, slopes_ref, q_ref, kn_ref, vn_ref, *rest, n_heads, pages_per_step,
                        past, lam_init):
    pg = pages_per_step
    ck_refs = rest[:pg]
    cv_refs = rest[pg:2 * pg]
    lam_ref, g_ref, o_ref, m_ref, l_ref, acc_ref = rest[2 * pg:]
    step = pl.program_id(1)
    ds = q_ref.shape[0]
    dv = 2 * HEAD_DIM

    @pl.when(step == 0)
    def _():
        m_ref[...] = jnp.full(m_ref.shape, NEG_INF, F32)
        l_ref[...] = jnp.zeros(l_ref.shape, F32)
        acc_ref[...] = jnp.zeros(acc_ref.shape, F32)

    row = lax.broadcasted_iota(I32, (ds, PAGE_SIZE), 0)
    col = lax.broadcasted_iota(I32, (ds, PAGE_SIZE), 1)
    dist0 = row - col + (past - step * (pg * PAGE_SIZE))
    dists = [(dist0 - g * PAGE_SIZE).astype(F32) for g in range(pg)]
    for h in range(n_heads):
        slope = slopes_ref[h]
        v_tiles = [cv_refs[g][:, h * dv:(h + 1) * dv].astype(BF16) for g in range(pg)]
        for mi in range(2):
            idx = 2 * h + mi
            c0 = idx * HEAD_DIM
            q_hm = q_ref[:, c0:c0 + HEAD_DIM]
            m_run, l_run, acc = m_ref[idx], l_ref[idx], acc_ref[idx]
            for g in range(pg):
                s = _dot_t(q_hm, ck_refs[g][:, c0:c0 + HEAD_DIM].astype(BF16)) - slope * dists[g]
                m_new = jnp.maximum(m_run, jnp.max(s, axis=-1, keepdims=True))
                corr = jnp.exp(m_run - m_new)
                p = jnp.exp(s - m_new)
                l_run = l_run * corr + jnp.sum(p, axis=-1, keepdims=True)
                acc = acc * corr + jnp.dot(p.astype(BF16), v_tiles[g], preferred_element_type=F32)
                m_run = m_new
            m_ref[idx], l_ref[idx], acc_ref[idx] = m_run, l_run, acc

    @pl.when(step == pl.num_programs(1) - 1)
    def _():
        r8 = lax.broadcasted_iota(I32, (ds, ds), 0)
        c8 = lax.broadcasted_iota(I32, (ds, ds), 1)
        for h in range(n_heads):
            slope = slopes_ref[h]
            v_new = vn_ref[:, h * dv:(h + 1) * dv]
            for mi in range(2):
                c0 = (2 * h + mi) * HEAD_DIM
                s = _dot_t(q_ref[:, c0:c0 + HEAD_DIM], kn_ref[:, c0:c0 + HEAD_DIM])
                s = s - slope * (r8 - c8).astype(F32)
                s = jnp.where(c8 <= r8, s, NEG_INF)
                _online_update(m_ref, l_ref, acc_ref, 2 * h + mi, s,
                               lambda p: jnp.dot(p.astype(BF16), v_new, preferred_element_type=F32))
            o1 = acc_ref[2 * h] / l_ref[2 * h]
            o2 = acc_ref[2 * h + 1] / l_ref[2 * h + 1]
            o_ref[:, h * dv:(h + 1) * dv] = _diff_combine(o1, o2, lam_ref, g_ref, lam_init).astype(o_ref.dtype)


def _diff_attn_sample(q, k_new, v_new, cache_k, cache_v, page_table, slopes, diff_lambda, subln_g,
                      layer, db, ds, lam_init, *, pages_per_step=4):
    dv = 2 * HEAD_DIM
    width = q.shape[1]
    n_heads = width // dv
    n_pages = page_table.shape[1]
    pg = math.gcd(pages_per_step, n_pages)
    ck = cache_k.reshape(cache_k.shape[0], cache_k.shape[1], PAGE_SIZE, width)
    cv = cache_v.reshape(cache_v.shape[0], cache_v.shape[1], PAGE_SIZE, width)

    def page_spec(g):
        return pl.BlockSpec((None, None, PAGE_SIZE, width),
                            lambda b, s, pt: (layer, pt[b, s * pg + g], 0, 0))

    row_spec = pl.BlockSpec((ds, width), lambda b, s, pt: (b, 0))
    kern = functools.partial(_diff_sample_kernel, n_heads=n_heads, pages_per_step=pg,
                             past=n_pages * PAGE_SIZE, lam_init=lam_init)
    grid_spec = pltpu.PrefetchScalarGridSpec(
        num_scalar_prefetch=1,
        grid=(db, n_pages // pg),
        in_specs=[pl.BlockSpec(memory_space=pltpu.SMEM), row_spec, row_spec, row_spec]
        + [page_spec(g) for g in range(pg)] + [page_spec(g) for g in range(pg)]
        + [pl.BlockSpec((None, 4, HEAD_DIM), lambda b, s, pt: (layer, 0, 0)),
           pl.BlockSpec((None, 1, dv), lambda b, s, pt: (layer, 0, 0))],
        out_specs=row_spec,
        scratch_shapes=[pltpu.VMEM((2 * n_heads, ds, 1), F32), pltpu.VMEM((2 * n_heads, ds, 1), F32),
                        pltpu.VMEM((2 * n_heads, ds, dv), F32)],
    )
    return pl.pallas_call(
        kern,
        grid_spec=grid_spec,
        out_shape=jax.ShapeDtypeStruct(q.shape, BF16),
        compiler_params=_params(("arbitrary", "arbitrary")),
    )(page_table, slopes, q, k_new, v_new, *([ck] * pg), *([cv] * pg), diff_lambda,
      subln_g.reshape(subln_g.shape[0], 1, dv))


def _top_blocks(gate, n_valid, ksel):
    nb = gate.shape[-1]
    lane = lax.broadcasted_iota(I32, gate.shape, 1)
    lane_f = lane.astype(F32)
    valid = lane < n_valid
    g = jnp.where(valid, gate, NEG_INF)
    picks = []
    for _ in range(ksel):
        mx = jnp.max(g, axis=-1, keepdims=True)
        idx = jnp.min(jnp.where((g == mx) & valid, lane_f, float(nb)), axis=-1, keepdims=True)
        picks.append(idx)
        hit = lane_f == idx
        g = jnp.where(hit, NEG_INF, g)
        valid = valid & jnp.logical_not(hit)
    return picks


def _block_mean_kernel(k_ref, o_ref):
    o_ref[...] = jnp.mean(k_ref[...], axis=0, keepdims=True)


def _block_means(k32, rows):
    t, width = k32.shape
    out = pl.pallas_call(
        _block_mean_kernel,
        grid=(t // rows,),
        in_specs=[pl.BlockSpec((rows, width), lambda i: (i, 0))],
        out_specs=pl.BlockSpec((None, 1, width), lambda i: (i, 0, 0)),
        out_shape=jax.ShapeDtypeStruct((t // rows, 1, width), F32),
        compiler_params=_params(("arbitrary",)),
    )(k32)
    return out.reshape(t // rows, width)


def _moba_prompt_kernel(slopes_ref, q_ref, k_ref, v_ref, km_ref, o_ref, m_ref, l_ref, acc_ref, *, ksel):
    h = pl.program_id(1)
    qi = pl.program_id(2)
    blk = MOBA_BLOCK
    slope = slopes_ref[h]
    q16 = q_ref[...]
    row = lax.broadcasted_iota(I32, (blk, blk), 0)
    col = lax.broadcasted_iota(I32, (blk, blk), 1)
    rel = (col - row).astype(F32) * slope

    d0 = pl.multiple_of(qi * blk, blk)
    s = _dot_t(q16, k_ref[pl.ds(d0, blk), :]) + rel
    s = jnp.where(col <= row, s, NEG_INF)
    m0 = jnp.max(s, axis=-1, keepdims=True)
    p = jnp.exp(s - m0)
    m_ref[0] = m0
    l_ref[0] = jnp.sum(p, axis=-1, keepdims=True)
    acc_ref[0] = jnp.dot(p.astype(BF16), v_ref[pl.ds(d0, blk), :], preferred_element_type=F32)

    gate = _dot_t(q16, km_ref[...].astype(BF16))
    picks = _top_blocks(gate, qi, ksel)

    def body(kj, c):
        k0 = pl.multiple_of(kj * blk, blk)
        kj_f = kj.astype(F32)
        chosen = picks[0] == kj_f
        for idx in picks[1:]:
            chosen = chosen | (idx == kj_f)
        row_bias = jnp.where(chosen, 0.0, NEG_INF) - slope * ((qi - kj) * blk).astype(F32)
        s = _dot_t(q16, k_ref[pl.ds(k0, blk), :]) + rel + row_bias
        v_t = v_ref[pl.ds(k0, blk), :]
        _online_update(m_ref, l_ref, acc_ref, 0, s,
                       lambda p: jnp.dot(p.astype(BF16), v_t, preferred_element_type=F32))
        return c

    lax.fori_loop(0, qi, body, 0)
    o_ref[...] = (acc_ref[0] / l_ref[0]).astype(o_ref.dtype)


def _moba_prompt(q16, k16, v16, kmean, slopes, bsz, seq):
    width = q16.shape[1]
    n_heads = width // HEAD_DIM
    assert seq % MOBA_BLOCK == 0
    nb = seq // MOBA_BLOCK
    ksel = min(MOBA_TOPK, nb - 1)
    if ksel == 0:
        ksel = 1
    kern = functools.partial(_moba_prompt_kernel, ksel=ksel)
    return pl.pallas_call(
        kern,
        grid=(bsz, n_heads, nb),
        in_specs=[
            pl.BlockSpec(memory_space=pltpu.SMEM),
            pl.BlockSpec((MOBA_BLOCK, HEAD_DIM), lambda b, h, i: (b * nb + i, h)),
            pl.BlockSpec((seq, HEAD_DIM), lambda b, h, i: (b, h)),
            pl.BlockSpec((seq, HEAD_DIM), lambda b, h, i: (b, h)),
            pl.BlockSpec((None, nb, HEAD_DIM), lambda b, h, i: (b, 0, h)),
        ],
        out_specs=pl.BlockSpec((MOBA_BLOCK, HEAD_DIM), lambda b, h, i: (b * nb + i, h)),
        out_shape=jax.ShapeDtypeStruct(q16.shape, BF16),
        scratch_shapes=[pltpu.VMEM((1, MOBA_BLOCK, 1), F32), pltpu.VMEM((1, MOBA_BLOCK, 1), F32),
                        pltpu.VMEM((1, MOBA_BLOCK, HEAD_DIM), F32)],
        compiler_params=_params(("arbitrary", "arbitrary", "arbitrary")),
    )(slopes, q16, k16, v16, kmean)


def _page_block_mean_kernel(pt_ref, *refs, bp):
    o_ref = refs[bp]
    tot = None
    for g in range(bp):
        part = jnp.sum(refs[g][...], axis=0, keepdims=True)
        tot = part if tot is None else tot + part
    o_ref[...] = tot * (1.0 / (bp * PAGE_SIZE))


def _paged_block_means(cache_k, page_table, layer, n_full):
    db = page_table.shape[0]
    bp = MOBA_BLOCK // PAGE_SIZE
    width = cache_k.shape[3] * cache_k.shape[4]
    ck = cache_k.reshape(cache_k.shape[0], cache_k.shape[1], PAGE_SIZE, width)

    def page_spec(g):
        return pl.BlockSpec((None, None, PAGE_SIZE, width),
                            lambda b, n, pt: (layer, pt[b, n * bp + g], 0, 0))

    grid_spec = pltpu.PrefetchScalarGridSpec(
        num_scalar_prefetch=1,
        grid=(db, n_full),
        in_specs=[page_spec(g) for g in range(bp)],
        out_specs=pl.BlockSpec((None, None, 1, width), lambda b, n, pt: (b, n, 0, 0)),
    )
    out = pl.pallas_call(
        functools.partial(_page_block_mean_kernel, bp=bp),
        grid_spec=grid_spec,
        out_shape=jax.ShapeDtypeStruct((db, n_full, 1, width), F32),
        compiler_params=_params(("arbitrary", "arbitrary")),
    )(page_table, *([ck] * bp))
    return out.reshape(db, n_full, width)


def _moba_select_kernel(q_ref, km_ref, o_ref, *, n_heads, ksel):
    n_full = km_ref.shape[0]
    lane = lax.broadcasted_iota(I32, (q_ref.shape[0], 128), 1)
    for h in range(n_heads):
        c0 = h * HEAD_DIM
        gate = _dot_t(q_ref[:, c0:c0 + HEAD_DIM], km_ref[:, c0:c0 + HEAD_DIM].astype(BF16))
        picks = _top_blocks(gate, n_full, ksel)
        out = jnp.zeros(lane.shape, I32)
        for j, idx in enumerate(picks):
            out = jnp.where(lane == j, idx.astype(I32), out)
        o_ref[h] = out


def _moba_select(q16, kmean, db, ds, ksel):
    width = q16.shape[1]
    n_heads = width // HEAD_DIM
    n_full = kmean.shape[1]
    out = pl.pallas_call(
        functools.partial(_moba_select_kernel, n_heads=n_heads, ksel=ksel),
        grid=(db,),
        in_specs=[pl.BlockSpec((ds, width), lambda b: (b, 0)),
                  pl.BlockSpec((None, n_full, width), lambda b: (b, 0, 0))],
        out_specs=pl.BlockSpec((None, n_heads, ds, 128), lambda b: (b, 0, 0, 0)),
        out_shape=jax.ShapeDtypeStruct((db, n_heads, ds, 128), I32),
        compiler_params=_params(("arbitrary",)),
    )(q16, kmean)
    return out[..., :ksel]


def _moba_sample_kernel(pt_ref, sel_ref, slopes_ref, q_ref, kn_ref, vn_ref, *rest, ksel, bp, past,
                        n_heads):
    n_pg = ksel * bp
    ck_refs = rest[:n_pg]
    cv_refs = rest[n_pg:2 * n_pg]
    o_ref, acc_ref = rest[2 * n_pg:]
    b = pl.program_id(0)
    h = pl.program_id(1)
    qn = pl.program_id(2)
    ds = q_ref.shape[0]
    slope = slopes_ref[h]
    q16 = q_ref[...]
    row = lax.broadcasted_iota(I32, (ds, PAGE_SIZE), 0)
    col = lax.broadcasted_iota(I32, (ds, PAGE_SIZE), 1)
    base = ((b * n_heads + h) * ds + qn) * ksel
    parts = []
    for j in range(ksel):
        blk_start = sel_ref[base + j] * MOBA_BLOCK
        for g in range(bp):
            dist = (row - col + (past - blk_start - g * PAGE_SIZE)).astype(F32)
            parts.append(_dot_t(q16, ck_refs[j * bp + g][...].astype(BF16)) - slope * dist)
    r8 = lax.broadcasted_iota(I32, (ds, ds), 0)
    c8 = lax.broadcasted_iota(I32, (ds, ds), 1)
    s_own = _dot_t(q16, kn_ref[...]) - slope * (r8 - c8).astype(F32)
    s_own = jnp.where(c8 <= r8, s_own, NEG_INF)
    mx = jnp.max(s_own, axis=-1, keepdims=True)
    for s in parts:
        mx = jnp.maximum(mx, jnp.max(s, axis=-1, keepdims=True))
    p_own = jnp.exp(s_own - mx)
    ps = [jnp.exp(s - mx) for s in parts]
    den = jnp.sum(p_own, axis=-1, keepdims=True)
    for p in ps:
        den = den + jnp.sum(p, axis=-1, keepdims=True)
    out = jnp.dot((p_own / den).astype(BF16), vn_ref[...], preferred_element_type=F32)
    for i, p in enumerate(ps):
        out = out + jnp.dot((p / den).astype(BF16), cv_refs[i][...].astype(BF16),
                            preferred_element_type=F32)
    @pl.when(qn == 0)
    def _():
        acc_ref[...] = jnp.zeros(acc_ref.shape, F32)

    rsel = lax.broadcasted_iota(I32, (ds, HEAD_DIM), 0) == qn
    acc_ref[...] = jnp.where(rsel, out, acc_ref[...])

    @pl.when(qn == ds - 1)
    def _():
        o_ref[...] = acc_ref[...].astype(o_ref.dtype)


def _moba_sample(q16, k_new, v_new, cache_k, cache_v, page_table, sel, slopes, layer, db, ds):
    width = q16.shape[1]
    n_heads = width // HEAD_DIM
    ksel = sel.shape[-1]
    bp = MOBA_BLOCK // PAGE_SIZE
    n_pages = page_table.shape[1]
    ck = cache_k.reshape(cache_k.shape[0], cache_k.shape[1], PAGE_SIZE, width)
    cv = cache_v.reshape(cache_v.shape[0], cache_v.shape[1], PAGE_SIZE, width)

    def page_spec(j, g):
        def imap(b, h, qn, pt, sl):
            blk = sl[((b * n_heads + h) * ds + qn) * ksel + j]
            return (layer, pt[b, blk * bp + g], 0, h)
        return pl.BlockSpec((None, None, PAGE_SIZE, HEAD_DIM), imap)

    head_spec = pl.BlockSpec((ds, HEAD_DIM), lambda b, h, qn, pt, sl: (b, h))
    specs = [page_spec(j, g) for j in range(ksel) for g in range(bp)]
    grid_spec = pltpu.PrefetchScalarGridSpec(
        num_scalar_prefetch=2,
        grid=(db, n_heads, ds),
        in_specs=[pl.BlockSpec(memory_space=pltpu.SMEM), head_spec, head_spec, head_spec] + specs + specs,
        out_specs=head_spec,
        scratch_shapes=[pltpu.VMEM((ds, HEAD_DIM), F32)],
    )
    kern = functools.partial(_moba_sample_kernel, ksel=ksel, bp=bp, past=n_pages * PAGE_SIZE,
                             n_heads=n_heads)
    n_pg = ksel * bp
    return pl.pallas_call(
        kern,
        grid_spec=grid_spec,
        out_shape=jax.ShapeDtypeStruct(q16.shape, BF16),
        compiler_params=_params(("arbitrary", "arbitrary", "arbitrary")),
    )(page_table, sel.reshape(-1), slopes, q16, k_new, v_new, *([ck] * n_pg), *([cv] * n_pg))


def _layer_norm_rows(z, g_ref, b_ref):
    mu = jnp.mean(z, axis=-1, keepdims=True)
    zc = z - mu
    var = jnp.mean(zc * zc, axis=-1, keepdims=True)
    return zc * lax.rsqrt(var + LN_EPS) * g_ref[...] + b_ref[...]


def _route(logits):
    lane = lax.broadcasted_iota(I32, logits.shape, 1)
    lane_f = lane.astype(F32)
    none = float(ROUTER_LANES)
    is_g = lane < MOE_GROUPS
    lg = jnp.where(is_g, logits, NEG_INF)
    mg = jnp.max(lg, axis=-1, keepdims=True)
    g_top = jnp.min(jnp.where(lg == mg, lane_f, none), axis=-1, keepdims=True)
    gate_g = 1.0 / jnp.sum(jnp.where(is_g, jnp.exp(logits - mg), 0.0), axis=-1, keepdims=True)
    e_lo = MOE_GROUPS + g_top * MOE_PER_GROUP
    le = jnp.where((lane_f >= e_lo) & (lane_f < e_lo + MOE_PER_GROUP), logits, NEG_INF)
    v1 = jnp.max(le, axis=-1, keepdims=True)
    i1 = jnp.min(jnp.where(le == v1, lane_f, none), axis=-1, keepdims=True)
    le2 = jnp.where(lane_f == i1, NEG_INF, le)
    v2 = jnp.max(le2, axis=-1, keepdims=True)
    i2 = jnp.min(jnp.where(le2 == v2, lane_f, none), axis=-1, keepdims=True)
    e2 = jnp.exp(v2 - v1)
    w1 = gate_g / (1.0 + e2)
    w2 = gate_g * e2 / (1.0 + e2)
    eid = jnp.where(lane == 0, i1 - MOE_GROUPS, jnp.where(lane == 1, i2 - MOE_GROUPS, 0.0)).astype(I32)
    wgt = jnp.where(lane == 0, w1, jnp.where(lane == 1, w2, 0.0))
    return eid, wgt


def _ln_router_kernel(z_ref, g_ref, b_ref, rw_ref, rb_ref, y32_ref, y16_ref, eid_ref, wgt_ref):
    y = _layer_norm_rows(z_ref[...], g_ref, b_ref)
    y32_ref[...] = y
    y16_ref[...] = y.astype(BF16)
    logits = jnp.dot(y.astype(BF16), rw_ref[...], preferred_element_type=F32) + rb_ref[...]
    eid, wgt = _route(logits)
    eid_ref[...] = eid
    wgt_ref[...] = wgt


def _ln_router(z, ln_g, ln_b, layer, which, rw, rb, *, tm=256):
    t, d = z.shape
    tm = min(tm, t)
    assert t % tm == 0
    row = pl.BlockSpec((tm, d), lambda i: (i, 0))
    vec = pl.BlockSpec((None, None, 1, d), lambda i: (layer, which, 0, 0))
    lane_blk = pl.BlockSpec((tm, ROUTER_LANES), lambda i: (i, 0))
    return pl.pallas_call(
        _ln_router_kernel,
        grid=(t // tm,),
        in_specs=[row, vec, vec, pl.BlockSpec((d, ROUTER_LANES), lambda i: (0, 0)),
                  pl.BlockSpec((1, ROUTER_LANES), lambda i: (0, 0))],
        out_specs=[row, row, lane_blk, lane_blk],
        out_shape=[jax.ShapeDtypeStruct((t, d), F32), jax.ShapeDtypeStruct((t, d), BF16),
                   jax.ShapeDtypeStruct((t, ROUTER_LANES), I32),
                   jax.ShapeDtypeStruct((t, ROUTER_LANES), F32)],
        compiler_params=_params(("arbitrary",)),
    )(z, ln_g.reshape(ln_g.shape[0], ln_g.shape[1], 1, d), ln_b.reshape(ln_b.shape[0], ln_b.shape[1], 1, d),
      rw, rb)


def _ln_combine_kernel(x_ref, y0_ref, y1_ref, w_ref, g_ref, b_ref, y32_ref, y16_ref, *, alpha):
    w = w_ref[...]
    z = alpha * x_ref[...] + (w[:, 0:1] * y0_ref[...] + w[:, 1:2] * y1_ref[...])
    y = _layer_norm_rows(z, g_ref, b_ref)
    y32_ref[...] = y
    y16_ref[...] = y.astype(BF16)


def _ln_combine(x, y0, y1, wgt, ln_g, ln_b, layer, which, alpha, *, tm=256):
    t, d = x.shape
    tm = min(tm, t)
    assert t % tm == 0
    row = pl.BlockSpec((tm, d), lambda i: (i, 0))
    vec = pl.BlockSpec((None, None, 1, d), lambda i: (layer, which, 0, 0))
    return pl.pallas_call(
        functools.partial(_ln_combine_kernel, alpha=alpha),
        grid=(t // tm,),
        in_specs=[row, row, row, pl.BlockSpec((tm, ROUTER_LANES), lambda i: (i, 0)), vec, vec],
        out_specs=[row, row],
        out_shape=[jax.ShapeDtypeStruct((t, d), F32), jax.ShapeDtypeStruct((t, d), BF16)],
        compiler_params=_params(("arbitrary",)),
    )(x, y0, y1, wgt, ln_g.reshape(ln_g.shape[0], ln_g.shape[1], 1, d),
      ln_b.reshape(ln_b.shape[0], ln_b.shape[1], 1, d))


def _moe_kernel(te_ref, nv_ref, x_ref, w1_ref, w3_ref, w2_ref, o_ref):
    t = pl.program_id(0)
    c = pl.program_id(1)

    @pl.when(t < nv_ref[0])
    def _():
        x = x_ref[...]
        a = jnp.dot(x, w1_ref[...].astype(BF16), preferred_element_type=F32)
        b = jnp.dot(x, w3_ref[...].astype(BF16), preferred_element_type=F32)
        hid = (a * jax.nn.sigmoid(a)) * b
        y = jnp.dot(hid.astype(BF16), w2_ref[...].astype(BF16), preferred_element_type=F32)

        @pl.when(c == 0)
        def _():
            o_ref[...] = y

        @pl.when(c != 0)
        def _():
            o_ref[...] += y

    @pl.when(t >= nv_ref[0])
    def _():
        o_ref[...] = jnp.zeros(o_ref.shape, o_ref.dtype)


def _moe_experts(xb, tile_expert, n_valid, w1, w3, w2, layer, *, n_chunks=2):
    n_slots, d = xb.shape
    n_tiles = n_slots // MOE_TILE
    d_exp = w1.shape[-1]
    ce = d_exp // n_chunks

    def chunk(t, c, nv):
        return jnp.where(t >= nv[0], 0, jnp.where(t % 2 == 0, c, n_chunks - 1 - c))

    grid_spec = pltpu.PrefetchScalarGridSpec(
        num_scalar_prefetch=2,
        grid=(n_tiles, n_chunks),
        in_specs=[
            pl.BlockSpec((MOE_TILE, d), lambda t, c, te, nv: (t, 0)),
            pl.BlockSpec((None, None, d, ce), lambda t, c, te, nv: (layer, te[t], 0, chunk(t, c, nv))),
            pl.BlockSpec((None, None, d, ce), lambda t, c, te, nv: (layer, te[t], 0, chunk(t, c, nv))),
            pl.BlockSpec((None, None, ce, d), lambda t, c, te, nv: (layer, te[t], chunk(t, c, nv), 0)),
        ],
        out_specs=pl.BlockSpec((MOE_TILE, d), lambda t, c, te, nv: (t, 0)),
    )
    return pl.pallas_call(
        _moe_kernel,
        grid_spec=grid_spec,
        out_shape=jax.ShapeDtypeStruct((n_slots, d), F32),
        compiler_params=_params(("arbitrary", "arbitrary")),
    )(tile_expert, n_valid, xb, w1, w3, w2)


def _moe_layer(x32, x16, eid, wgt, ln_g, ln_b, layer, w1, w3, w2, alpha):
    n_tok, d = x32.shape
    n_asg = n_tok * MOE_TOPK
    e_flat = eid[:, :MOE_TOPK].reshape(n_asg)
    onehot = (e_flat[:, None] == jnp.arange(N_EXPERTS, dtype=I32)[None, :]).astype(I32)
    csum = jnp.cumsum(onehot, axis=0)
    rank = jnp.sum(csum * onehot, axis=1) - 1
    counts = csum[-1]
    padded = (counts + MOE_TILE - 1) // MOE_TILE * MOE_TILE
    pad_ends = jnp.cumsum(padded)
    dest = (pad_ends - padded)[e_flat] + rank
    n_tiles = -(-(n_asg + N_EXPERTS * (MOE_TILE - 1)) // MOE_TILE)
    slot_tok = jnp.full((n_tiles * MOE_TILE,), n_tok, I32).at[dest].set(jnp.arange(n_asg, dtype=I32) // MOE_TOPK)
    tile_start = jnp.arange(n_tiles, dtype=I32) * MOE_TILE
    n_valid = (pad_ends[-1] // MOE_TILE).astype(I32)
    tile_expert = jnp.searchsorted(pad_ends, jnp.minimum(tile_start, pad_ends[-1] - 1), side="right")
    tile_expert = jnp.minimum(tile_expert, N_EXPERTS - 1).astype(I32)
    xb = jnp.take(x16, slot_tok, axis=0, mode="fill", fill_value=0)
    yb = _moe_experts(xb, tile_expert, n_valid.reshape(1), w1, w3, w2, layer)
    dest2 = dest.reshape(n_tok, MOE_TOPK)
    y0 = jnp.take(yb, dest2[:, 0], axis=0)
    y1 = jnp.take(yb, dest2[:, 1], axis=0)
    return _ln_combine(x32, y0, y1, wgt, ln_g, ln_b, layer, 1, alpha, tm=_row_tile(n_tok, 256))


def _router_weights(router_g_w, router_g_b, router_e_w, router_e_b, layer):
    d = router_g_w.shape[1]
    pad = ROUTER_LANES - MOE_GROUPS - N_EXPERTS
    rw = jnp.concatenate([router_g_w[layer], router_e_w[layer], jnp.zeros((d, pad), F32)], axis=1).astype(BF16)
    rb = jnp.concatenate([router_g_b[layer], router_e_b[layer], jnp.zeros((pad,), F32)])[None, :]
    return rw, rb


def kernel(x_prompt, x_sample, cache_k_diff, cache_v_diff, state_pool, cache_k_moba, cache_v_moba, page_table, w_in_even, pool_w, pool_scale, diff_lambda, diff_subln_g, w_out_even, w_in_odd, w_out_odd, ln_g, ln_b, router_g_w, router_g_b, router_e_w, router_e_b, moe_w1, moe_w3, moe_w2):
    bsz, seq, d = x_prompt.shape
    db, ds, _ = x_sample.shape
    depth = ln_g.shape[0]
    alpha = (2.0 * depth) ** 0.25
    n_pages = page_table.shape[1]
    past = n_pages * PAGE_SIZE
    tp, ts = bsz * seq, db * ds
    pool_width = pool_scale.shape[1]
    diff_width = cache_k_diff.shape[3] * cache_k_diff.shape[4]
    moba_width = cache_k_moba.shape[3] * cache_k_moba.shape[4]
    slopes_diff = jnp.asarray(_alibi_slopes(cache_k_diff.shape[3]))
    slopes_moba = jnp.asarray(_alibi_slopes(cache_k_moba.shape[3]))
    scale = HEAD_DIM ** -0.5
    tm_p = 512

    xp32, xs32 = x_prompt.reshape(tp, d), x_sample.reshape(ts, d)
    xp16, xs16 = xp32.astype(BF16), xs32.astype(BF16)
    kd_p, vd_p, pl_p, km_p, vm_p = [], [], [], [], []
    kd_s, vd_s, pl_s, km_s, vm_s = [], [], [], [], []
    for i in range(depth):
        j = i // 2
        if i % 2 == 0:
            lam_init = 0.8 - 0.6 * math.exp(-0.3 * i)
            zs = []
            for x16, x32, tm, is_prompt in ((xp16, xp32, tm_p, True), (xs16, xs32, ts, False)):
                t = x16.shape[0]
                (u32,) = _matmul([x16], w_in_even, j, 0, pool_width, [(F32, 1.0)], tm=tm)
                (q16,) = _matmul([x16], w_in_even, j, pool_width, diff_width, [(BF16, scale)], tm=tm)
                k32, k16 = _matmul([x16], w_in_even, j, pool_width + diff_width, diff_width,
                                   [(F32, 1.0), (BF16, 1.0)], tm=tm)
                v32, v16 = _matmul([x16], w_in_even, j, pool_width + 2 * diff_width, diff_width,
                                   [(F32, 1.0), (BF16, 1.0)], tm=tm)
                if is_prompt:
                    u3 = u32.reshape(bsz, seq, pool_width)
                    prefix = jnp.zeros((bsz, POOL_HALO, pool_width), F32)
                    pool = _pool_mix(u3, prefix, pool_w, pool_scale, j, 0, tl=256)
                    att = _diff_attn_prompt(q16, k16, v16, slopes_diff, diff_lambda, diff_subln_g, j,
                                            bsz, seq, lam_init)
                    ext = u3 if seq >= POOL_CARRY else jnp.concatenate([prefix[:, :POOL_CARRY], u3], axis=1)
                    kd_p.append(k32.reshape(bsz, seq, -1, 2 * HEAD_DIM))
                    vd_p.append(v32.reshape(bsz, seq, -1, 2 * HEAD_DIM))
                    pl_p.append(ext[:, -POOL_CARRY:])
                else:
                    u3 = u32.reshape(db, ds, pool_width)
                    prefix = jnp.concatenate([jnp.zeros((db, POOL_HALO - POOL_CARRY, pool_width), F32),
                                              state_pool[j]], axis=1)
                    pool = _pool_mix(u3, prefix, pool_w, pool_scale, j, past, tl=ds)
                    att = _diff_attn_sample(q16, k16, v16, cache_k_diff, cache_v_diff, page_table,
                                            slopes_diff, diff_lambda, diff_subln_g, j, db, ds, lam_init)
                    ext = jnp.concatenate([state_pool[j], u3], axis=1)
                    kd_s.append(k32.reshape(db, ds, -1, 2 * HEAD_DIM))
                    vd_s.append(v32.reshape(db, ds, -1, 2 * HEAD_DIM))
                    pl_s.append(ext[:, -POOL_CARRY:])
                (z,) = _matmul([pool, att], w_out_even, j, 0, d, [(F32, 1.0)], tm=tm, resid=x32, alpha=alpha)
                zs.append(z)
        else:
            zs = []
            for x16, x32, tm, is_prompt in ((xp16, xp32, tm_p, True), (xs16, xs32, ts, False)):
                (q16,) = _matmul([x16], w_in_odd, j, 0, moba_width, [(BF16, scale)], tm=tm)
                k32, k16 = _matmul([x16], w_in_odd, j, moba_width, moba_width, [(F32, 1.0), (BF16, 1.0)], tm=tm)
                v32, v16 = _matmul([x16], w_in_odd, j, 2 * moba_width, moba_width, [(F32, 1.0), (BF16, 1.0)], tm=tm)
                if is_prompt:
                    kmean = _block_means(k32, MOBA_BLOCK).reshape(bsz, seq // MOBA_BLOCK, moba_width)
                    att = _moba_prompt(q16, k16, v16, kmean, slopes_moba, bsz, seq)
                    km_p.append(k32.reshape(bsz, seq, -1, HEAD_DIM))
                    vm_p.append(v32.reshape(bsz, seq, -1, HEAD_DIM))
                else:
                    n_full = past // MOBA_BLOCK
                    assert past == n_full * MOBA_BLOCK and n_full >= 1
                    ksel = min(MOBA_TOPK, n_full)
                    kmean = _paged_block_means(cache_k_moba, page_table, j, n_full)
                    sel = _moba_select(q16, kmean, db, ds, ksel)
                    att = _moba_sample(q16, k16, v16, cache_k_moba, cache_v_moba, page_table,
                                       sel, slopes_moba, j, db, ds)
                    km_s.append(k32.reshape(db, ds, -1, HEAD_DIM))
                    vm_s.append(v32.reshape(db, ds, -1, HEAD_DIM))
                w_out = w_out_odd
                (z,) = _matmul([att], w_out, j, 0, d, [(F32, 1.0)], tm=tm, resid=x32, alpha=alpha)
                zs.append(z)
        rw, rb = _router_weights(router_g_w, router_g_b, router_e_w, router_e_b, i)
        outs = [_ln_router(z, ln_g, ln_b, i, 0, rw, rb) for z in zs]
        x32 = jnp.concatenate([o[0] for o in outs], axis=0)
        x16 = jnp.concatenate([o[1] for o in outs], axis=0)
        eid = jnp.concatenate([o[2] for o in outs], axis=0)
        wgt = jnp.concatenate([o[3] for o in outs], axis=0)
        y32, y16 = _moe_layer(x32, x16, eid, wgt, ln_g, ln_b, i, moe_w1, moe_w3, moe_w2, alpha)
        xp32, xs32 = y32[:tp], y32[tp:]
        xp16, xs16 = y16[:tp], y16[tp:]
    return (xp32.reshape(bsz, seq, d), xs32.reshape(db, ds, d),
            jnp.stack(kd_p), jnp.stack(vd_p), jnp.stack(pl_p), jnp.stack(km_p), jnp.stack(vm_p),
            jnp.stack(kd_s), jnp.stack(vd_s), jnp.stack(pl_s), jnp.stack(km_s), jnp.stack(vm_s))
```

```python
import functools
import math

import numpy as np
import jax
import jax.numpy as jnp
from jax import lax
from jax.experimental import pallas as pl
from jax.experimental.pallas import tpu as pltpu

F32 = jnp.float32
BF16 = jnp.bfloat16
I32 = jnp.int32

HEAD_DIM = 128
POOL_WINDOWS = (2, 4, 8, 16)
POOL_CARRY = max(POOL_WINDOWS) - 1
POOL_HALO = 16
MOBA_BLOCK = 256
MOBA_TOPK = 3
PAGE_SIZE = 128
MOE_GROUPS = 4
MOE_PER_GROUP = 8
N_EXPERTS = MOE_GROUPS * MOE_PER_GROUP
MOE_TOPK = 2
LN_EPS = 1e-5
RMS_EPS = 1e-5
NEG_INF = float("-inf")

MOE_TILE = 256
ROUTER_LANES = 128
VMEM_LIMIT = 56 * 1024 * 1024


def _alibi_slopes(n):
    def pow2(m):
        start = 2.0 ** (-(2.0 ** -(math.log2(m) - 3)))
        return [start ** (i + 1) for i in range(m)]
    if (n & (n - 1)) == 0:
        s = pow2(n)
    else:
        c = 2 ** int(math.floor(math.log2(n)))
        s = pow2(c) + pow2(2 * c)[0::2][: n - c]
    return np.asarray(s, np.float32)


def _row_tile(n, cap):
    best = max(t for t in range(16, cap + 1, 16) if n % t == 0)
    return best


def _params(semantics):
    return pltpu.CompilerParams(dimension_semantics=semantics, vmem_limit_bytes=VMEM_LIMIT)


def _dot_t(a, b, precision=None):
    return lax.dot_general(a, b, (((1,), (1,)), ((), ())), preferred_element_type=F32,
                           precision=precision)


def _bdot_t(a, b):
    return lax.dot_general(a, b, (((2,), (2,)), ((0,), (0,))), preferred_element_type=F32)


def _bdot(a, b):
    return lax.dot_general(a, b, (((2,), (1,)), ((0,), (0,))), preferred_element_type=F32)


def _mm_kernel(*refs, k_splits, n_out, has_resid, alpha, scales, cast_rows):
    n_x = len(k_splits)
    x_refs = refs[:n_x]
    w_ref = refs[n_x]
    pos = n_x + 1
    resid_ref = refs[pos] if has_resid else None
    pos += int(has_resid)
    out_refs = refs[pos:pos + n_out]
    wbf_ref = refs[pos + n_out]

    @pl.when(pl.program_id(1) == 0)
    def _():
        def body(r, c):
            rows = pl.ds(pl.multiple_of(r * cast_rows, cast_rows), cast_rows)
            wbf_ref[rows, :] = w_ref[rows, :].astype(BF16)
            return c
        lax.fori_loop(0, w_ref.shape[0] // cast_rows, body, 0)

    acc = None
    for x_ref, (k0, k1) in zip(x_refs, k_splits):
        part = jnp.dot(x_ref[...], wbf_ref[k0:k1, :], preferred_element_type=F32)
        acc = part if acc is None else acc + part
    if has_resid:
        acc = alpha * resid_ref[...] + acc
    for o_ref, sc in zip(out_refs, scales):
        o_ref[...] = (acc if sc == 1.0 else acc * sc).astype(o_ref.dtype)


def _matmul(xs, w, layer, col0, ncols, outs, *, tm, tn=512, resid=None, alpha=1.0):
    m = xs[0].shape[0]
    k_total = w.shape[1]
    tm = min(tm, m)
    assert m % tm == 0 and ncols % tn == 0 and col0 % tn == 0
    k_splits, k0 = [], 0
    for x in xs:
        k_splits.append((k0, k0 + x.shape[1]))
        k0 += x.shape[1]
    assert k0 == k_total
    cb = col0 // tn
    in_specs = [pl.BlockSpec((tm, x.shape[1]), lambda j, i: (i, 0)) for x in xs]
    in_specs.append(pl.BlockSpec((None, k_total, tn), lambda j, i: (layer, 0, cb + j)))
    args = list(xs) + [w]
    if resid is not None:
        in_specs.append(pl.BlockSpec((tm, tn), lambda j, i: (i, j)))
        args.append(resid)
    kern = functools.partial(
        _mm_kernel, k_splits=tuple(k_splits), n_out=len(outs), has_resid=resid is not None,
        alpha=alpha, scales=tuple(s for _, s in outs), cast_rows=min(512, k_total))
    res = pl.pallas_call(
        kern,
        grid=(ncols // tn, m // tm),
        in_specs=in_specs,
        out_specs=[pl.BlockSpec((tm, tn), lambda j, i: (i, j)) for _ in outs],
        out_shape=[jax.ShapeDtypeStruct((m, ncols), dt) for dt, _ in outs],
        scratch_shapes=[pltpu.VMEM((k_total, tn), BF16)],
        compiler_params=_params(("arbitrary", "arbitrary")),
    )(*args)
    return res


def _pool_kernel(u_ref, pre_ref, w_ref, sc_ref, o_ref, ext_ref, *, tl, pos0, group):
    li = pl.program_id(1)

    @pl.when(li == 0)
    def _():
        ext_ref[0:POOL_HALO, :] = pre_ref[...]

    ext_ref[POOL_HALO:POOL_HALO + tl, :] = u_ref[...]
    pos = pos0 + li * tl + lax.broadcasted_iota(I32, (tl, 1), 0)
    for g, win in enumerate(POOL_WINDOWS):
        c0, c1 = g * group, (g + 1) * group
        cur = ext_ref[POOL_HALO:POOL_HALO + tl, c0:c1]
        tot = cur
        for i in range(1, win):
            tot = tot + ext_ref[POOL_HALO - i:POOL_HALO - i + tl, c0:c1]
        cnt = jnp.minimum(win, pos + 1).astype(F32)
        pooled = tot / cnt - cur
        y = jnp.dot(pooled.astype(BF16), w_ref[g].astype(BF16), preferred_element_type=F32)
        o_ref[:, c0:c1] = (y * sc_ref[:, c0:c1]).astype(o_ref.dtype)
    if tl >= POOL_HALO:
        ext_ref[0:POOL_HALO, :] = ext_ref[tl:tl + POOL_HALO, :]


def _pool_mix(u, prefix, pool_w, pool_scale, layer, pos0, *, tl):
    bsz, length, width = u.shape
    tl = min(tl, length)
    assert length % tl == 0 and (length == tl or tl >= POOL_HALO)
    group = width // len(POOL_WINDOWS)
    kern = functools.partial(_pool_kernel, tl=tl, pos0=pos0, group=group)
    out = pl.pallas_call(
        kern,
        grid=(bsz, length // tl),
        in_specs=[
            pl.BlockSpec((None, tl, width), lambda b, l: (b, l, 0)),
            pl.BlockSpec((None, POOL_HALO, width), lambda b, l: (b, 0, 0)),
            pl.BlockSpec((None, len(POOL_WINDOWS), group, group), lambda b, l: (layer, 0, 0, 0)),
            pl.BlockSpec((None, 1, width), lambda b, l: (layer, 0, 0)),
        ],
        out_specs=pl.BlockSpec((None, tl, width), lambda b, l: (b, l, 0)),
        out_shape=jax.ShapeDtypeStruct((bsz, length, width), BF16),
        scratch_shapes=[pltpu.VMEM((POOL_HALO + tl, width), F32)],
        compiler_params=_params(("arbitrary", "arbitrary")),
    )(u, prefix, pool_w, pool_scale.reshape(pool_scale.shape[0], 1, width))
    return out.reshape(bsz * length, width)


def _diff_combine(o1, o2, lam_ref, g_ref, lam_init):
    lam = lam_ref[...]
    e1 = jnp.exp(jnp.sum(lam[0:1] * lam[1:2], axis=-1, keepdims=True))
    e2 = jnp.exp(jnp.sum(lam[2:3] * lam[3:4], axis=-1, keepdims=True))
    lam_full = e1 - e2 + lam_init
    d = o1 - lam_full * o2
    ms = jnp.mean(d * d, axis=-1, keepdims=True)
    return d * lax.rsqrt(ms + RMS_EPS) * g_ref[...] * (1.0 - lam_init)


def _diff_prompt_kernel(slopes_ref, q_ref, k_ref, v_ref, lam_ref, g_ref, o_ref, m_ref, l_ref, acc_ref, *,
                        tq, lam_init, group):
    qi = pl.program_id(2)
    dv = 2 * HEAD_DIM
    nmap = 2 * group
    slope3 = slopes_ref[...]
    row = lax.broadcasted_iota(I32, (tq, tq), 0)
    col = lax.broadcasted_iota(I32, (tq, tq), 1)
    rel = slope3 * (col - row).astype(F32)[None]
    q3 = _heads(q_ref, slice(None), nmap, HEAD_DIM)

    def values(rows):
        return jnp.stack([v_ref[rows, (j // 2) * dv:(j // 2 + 1) * dv] for j in range(nmap)], axis=0)

    d0 = pl.ds(pl.multiple_of(qi * tq, tq), tq)
    s = _bdot_t(q3, _heads(k_ref, d0, nmap, HEAD_DIM)) + rel
    s = jnp.where((col <= row)[None], s, NEG_INF)
    m0 = jnp.max(s, axis=-1, keepdims=True)
    p = jnp.exp(s - m0)
    m_ref[...] = m0
    l_ref[...] = jnp.sum(p, axis=-1, keepdims=True)
    acc_ref[...] = _bdot(p.astype(BF16), values(d0))

    def body(kj, c):
        k0 = pl.ds(pl.multiple_of(kj * tq, tq), tq)
        s = _bdot_t(q3, _heads(k_ref, k0, nmap, HEAD_DIM)) + (rel - slope3 * ((qi - kj) * tq).astype(F32))
        m_prev = m_ref[...]
        m_new = jnp.maximum(m_prev, jnp.max(s, axis=-1, keepdims=True))
        corr = jnp.exp(m_prev - m_new)
        p = jnp.exp(s - m_new)
        l_ref[...] = corr * l_ref[...] + jnp.sum(p, axis=-1, keepdims=True)
        acc_ref[...] = corr * acc_ref[...] + _bdot(p.astype(BF16), values(k0))
        m_ref[...] = m_new
        return c

    lax.fori_loop(0, qi, body, 0)
    o = acc_ref[...] / l_ref[...]
    for i in range(group):
        o_ref[:, i * dv:(i + 1) * dv] = _diff_combine(o[2 * i], o[2 * i + 1], lam_ref, g_ref,
                                                       lam_init).astype(o_ref.dtype)


def _diff_attn_prompt(q, k, v, slopes, diff_lambda, subln_g, layer, bsz, seq, lam_init, *, tq=256, group=2):
    dv = 2 * HEAD_DIM
    n_heads = q.shape[1] // dv
    tq = min(tq, seq)
    assert seq % tq == 0 and n_heads % group == 0
    nq = seq // tq
    gw = group * dv
    kern = functools.partial(_diff_prompt_kernel, tq=tq, lam_init=lam_init, group=group)
    return pl.pallas_call(
        kern,
        grid=(bsz, n_heads // group, nq),
        in_specs=[
            pl.BlockSpec((None, 2 * group, 1, 1), lambda b, h, i: (h, 0, 0, 0)),
            pl.BlockSpec((tq, gw), lambda b, h, i: (b * nq + i, h)),
            pl.BlockSpec((seq, gw), lambda b, h, i: (b, h)),
            pl.BlockSpec((seq, gw), lambda b, h, i: (b, h)),
            pl.BlockSpec((None, 4, HEAD_DIM), lambda b, h, i: (layer, 0, 0)),
            pl.BlockSpec((None, 1, dv), lambda b, h, i: (layer, 0, 0)),
        ],
        out_specs=pl.BlockSpec((tq, gw), lambda b, h, i: (b * nq + i, h)),
        out_shape=jax.ShapeDtypeStruct(q.shape, BF16),
        scratch_shapes=[pltpu.VMEM((2 * group, tq, 1), F32), pltpu.VMEM((2 * group, tq, 1), F32),
                        pltpu.VMEM((2 * group, tq, dv), F32)],
        compiler_params=_params(("arbitrary", "arbitrary", "arbitrary")),
    )(jnp.repeat(slopes, 2).reshape(n_heads // group, 2 * group, 1, 1), q, k, v, diff_lambda,
      subln_g.reshape(subln_g.shape[0], 1, dv))


def _diff_sample_kernel(pt_ref, slopes_ref, q_ref, kn_ref, vn_ref, *rest, pages_per_step, past, lam_init):
    pg = pages_per_step
    ck_refs = rest[:pg]
    cv_refs = rest[pg:2 * pg]
    lam_ref, g_ref, o_ref, m_ref, l_ref, acc_ref = rest[2 * pg:]
    step = pl.program_id(1)
    n_steps = pl.num_programs(1)
    ds = q_ref.shape[2]

    @pl.when(step == 0)
    def _():
        m_ref[...] = jnp.full(m_ref.shape, NEG_INF, F32)
        l_ref[...] = jnp.zeros(l_ref.shape, F32)
        acc_ref[...] = jnp.zeros(acc_ref.shape, F32)

    slope3 = slopes_ref[...]
    row = lax.broadcasted_iota(I32, (ds, PAGE_SIZE), 0)
    col = lax.broadcasted_iota(I32, (ds, PAGE_SIZE), 1)
    dist0 = row - col + (past - step * (pg * PAGE_SIZE))
    qs = (q_ref[0], q_ref[1])
    m_run = [m_ref[0], m_ref[1]]
    l_run = [l_ref[0], l_ref[1]]
    acc = [acc_ref[0], acc_ref[1]]
    for g in range(pg):
        bias = slope3 * (dist0 - g * PAGE_SIZE).astype(F32)[None]
        k3 = ck_refs[g][...].astype(BF16)
        v3 = cv_refs[g][...].astype(BF16)
        for mi in range(2):
            s = _bdot_t(qs[mi], k3[:, :, mi * HEAD_DIM:(mi + 1) * HEAD_DIM]) - bias
            m_new = jnp.maximum(m_run[mi], jnp.max(s, axis=-1, keepdims=True))
            corr = jnp.exp(m_run[mi] - m_new)
            p = jnp.exp(s - m_new)
            l_run[mi] = l_run[mi] * corr + jnp.sum(p, axis=-1, keepdims=True)
            acc[mi] = acc[mi] * corr + _bdot(p.astype(BF16), v3)
            m_run[mi] = m_new
    for mi in range(2):
        m_ref[mi], l_ref[mi], acc_ref[mi] = m_run[mi], l_run[mi], acc[mi]

    @pl.when(step == n_steps - 1)
    def _():
        r8 = lax.broadcasted_iota(I32, (ds, ds), 0)
        c8 = lax.broadcasted_iota(I32, (ds, ds), 1)
        bias8 = slope3 * (r8 - c8).astype(F32)[None]
        outs = []
        for mi in range(2):
            s = _bdot_t(qs[mi], kn_ref[mi]) - bias8
            s = jnp.where((c8 <= r8)[None], s, NEG_INF)
            m_new = jnp.maximum(m_run[mi], jnp.max(s, axis=-1, keepdims=True))
            corr = jnp.exp(m_run[mi] - m_new)
            p = jnp.exp(s - m_new)
            l_fin = l_run[mi] * corr + jnp.sum(p, axis=-1, keepdims=True)
            a_fin = acc[mi] * corr + _bdot(p.astype(BF16), vn_ref[...])
            outs.append(a_fin / l_fin)
        o_ref[...] = _diff_combine(outs[0], outs[1], lam_ref, g_ref, lam_init).astype(o_ref.dtype)


def _diff_attn_sample(q, k_new, v_new, cache_k, cache_v, page_table, slopes, diff_lambda, subln_g,
                      layer, db, ds, lam_init, *, pages_per_step=4):
    dv = 2 * HEAD_DIM
    width = q.shape[1]
    n_heads = width // dv
    n_pages = page_table.shape[1]
    pg = math.gcd(pages_per_step, n_pages)
    q5 = q.reshape(db, ds, n_heads, 2, HEAD_DIM).transpose(0, 3, 2, 1, 4)
    kn5 = k_new.reshape(db, ds, n_heads, 2, HEAD_DIM).transpose(0, 3, 2, 1, 4)
    vn4 = v_new.reshape(db, ds, n_heads, dv).transpose(0, 2, 1, 3)
    map_spec = pl.BlockSpec((None, 2, n_heads, ds, HEAD_DIM), lambda b, s, pt: (b, 0, 0, 0, 0))
    head_spec = pl.BlockSpec((None, n_heads, ds, dv), lambda b, s, pt: (b, 0, 0, 0))
    ck = cache_k.transpose(0, 1, 3, 2, 4)
    cv = cache_v.transpose(0, 1, 3, 2, 4)

    def page_spec(g):
        return pl.BlockSpec((None, None, n_heads, PAGE_SIZE, dv),
                            lambda b, s, pt: (layer, pt[b, s * pg + g], 0, 0, 0))

    kern = functools.partial(_diff_sample_kernel, pages_per_step=pg, past=n_pages * PAGE_SIZE,
                             lam_init=lam_init)
    grid_spec = pltpu.PrefetchScalarGridSpec(
        num_scalar_prefetch=1,
        grid=(db, n_pages // pg),
        in_specs=[pl.BlockSpec((n_heads, 1, 1), lambda b, s, pt: (0, 0, 0)), map_spec, map_spec, head_spec]
        + [page_spec(g) for g in range(pg)] + [page_spec(g) for g in range(pg)]
        + [pl.BlockSpec((None, 4, HEAD_DIM), lambda b, s, pt: (layer, 0, 0)),
           pl.BlockSpec((None, 1, dv), lambda b, s, pt: (layer, 0, 0))],
        out_specs=head_spec,
        scratch_shapes=[pltpu.VMEM((2, n_heads, ds, 1), F32), pltpu.VMEM((2, n_heads, ds, 1), F32),
                        pltpu.VMEM((2, n_heads, ds, dv), F32)],
    )
    out = pl.pallas_call(
        kern,
        grid_spec=grid_spec,
        out_shape=jax.ShapeDtypeStruct((db, n_heads, ds, dv), BF16),
        compiler_params=_params(("arbitrary", "arbitrary")),
    )(page_table, slopes.reshape(n_heads, 1, 1), q5, kn5, vn4, *([ck] * pg), *([cv] * pg), diff_lambda,
      subln_g.reshape(subln_g.shape[0], 1, dv))
    return out.transpose(0, 2, 1, 3).reshape(db * ds, width)


def _top_blocks(gate, n_valid, ksel):
    nb = gate.shape[-1]
    lane = lax.broadcasted_iota(I32, gate.shape, gate.ndim - 1)
    lane_f = lane.astype(F32)
    valid = lane < n_valid
    g = jnp.where(valid, gate, NEG_INF)
    picks = []
    for _ in range(ksel):
        mx = jnp.max(g, axis=-1, keepdims=True)
        idx = jnp.min(jnp.where((g == mx) & valid, lane_f, float(nb)), axis=-1, keepdims=True)
        picks.append(idx)
        hit = lane_f == idx
        g = jnp.where(hit, NEG_INF, g)
        valid = valid & jnp.logical_not(hit)
    return picks


def _block_mean_kernel(k_ref, o_ref):
    o_ref[...] = jnp.mean(k_ref[...], axis=0, keepdims=True)


def _block_means(k32, rows):
    t, width = k32.shape
    out = pl.pallas_call(
        _block_mean_kernel,
        grid=(t // rows,),
        in_specs=[pl.BlockSpec((rows, width), lambda i: (i, 0))],
        out_specs=pl.BlockSpec((None, 1, width), lambda i: (i, 0, 0)),
        out_shape=jax.ShapeDtypeStruct((t // rows, 1, width), F32),
        compiler_params=_params(("arbitrary",)),
    )(k32)
    return out.reshape(t // rows, width)


def _heads(ref, rows, n, width):
    return jnp.stack([ref[rows, i * width:(i + 1) * width] for i in range(n)], axis=0)


def _moba_prompt_kernel(slopes_ref, q_ref, k_ref, v_ref, km_ref, o_ref, m_ref, l_ref, acc_ref, *, ksel,
                        group):
    qi = pl.program_id(2)
    blk = MOBA_BLOCK
    slope3 = slopes_ref[...]
    q3 = _heads(q_ref, slice(None), group, HEAD_DIM)
    row = lax.broadcasted_iota(I32, (blk, blk), 0)
    col = lax.broadcasted_iota(I32, (blk, blk), 1)
    rel = slope3 * (col - row).astype(F32)[None]

    d0 = pl.ds(pl.multiple_of(qi * blk, blk), blk)
    s = _bdot_t(q3, _heads(k_ref, d0, group, HEAD_DIM)) + rel
    s = jnp.where((col <= row)[None], s, NEG_INF)
    m0 = jnp.max(s, axis=-1, keepdims=True)
    p = jnp.exp(s - m0)
    m_ref[...] = m0
    l_ref[...] = jnp.sum(p, axis=-1, keepdims=True)
    acc_ref[...] = _bdot(p.astype(BF16), _heads(v_ref, d0, group, HEAD_DIM))

    gate = _bdot_t(q3, _heads(km_ref, slice(None), group, HEAD_DIM).astype(BF16))
    picks = _top_blocks(gate, qi, ksel)

    def body(kj, c):
        k0 = pl.ds(pl.multiple_of(kj * blk, blk), blk)
        kj_f = kj.astype(F32)
        chosen = picks[0] == kj_f
        for idx in picks[1:]:
            chosen = chosen | (idx == kj_f)
        row_bias = jnp.where(chosen, 0.0, NEG_INF) - slope3 * ((qi - kj) * blk).astype(F32)
        s = _bdot_t(q3, _heads(k_ref, k0, group, HEAD_DIM)) + rel + row_bias
        m_prev = m_ref[...]
        m_new = jnp.maximum(m_prev, jnp.max(s, axis=-1, keepdims=True))
        corr = jnp.exp(m_prev - m_new)
        p = jnp.exp(s - m_new)
        l_ref[...] = corr * l_ref[...] + jnp.sum(p, axis=-1, keepdims=True)
        acc_ref[...] = corr * acc_ref[...] + _bdot(p.astype(BF16), _heads(v_ref, k0, group, HEAD_DIM))
        m_ref[...] = m_new
        return c

    lax.fori_loop(0, qi, body, 0)
    out = acc_ref[...] / l_ref[...]
    for i in range(group):
        o_ref[:, i * HEAD_DIM:(i + 1) * HEAD_DIM] = out[i].astype(o_ref.dtype)


def _moba_prompt(q16, k16, v16, kmean, slopes, bsz, seq, *, group=4):
    width = q16.shape[1]
    n_heads = width // HEAD_DIM
    assert seq % MOBA_BLOCK == 0 and n_heads % group == 0
    nb = seq // MOBA_BLOCK
    gw = group * HEAD_DIM
    ksel = min(MOBA_TOPK, nb - 1)
    if ksel == 0:
        ksel = 1
    kern = functools.partial(_moba_prompt_kernel, ksel=ksel, group=group)
    return pl.pallas_call(
        kern,
        grid=(bsz, n_heads // group, nb),
        in_specs=[
            pl.BlockSpec((None, group, 1, 1), lambda b, h, i: (h, 0, 0, 0)),
            pl.BlockSpec((MOBA_BLOCK, gw), lambda b, h, i: (b * nb + i, h)),
            pl.BlockSpec((seq, gw), lambda b, h, i: (b, h)),
            pl.BlockSpec((seq, gw), lambda b, h, i: (b, h)),
            pl.BlockSpec((None, nb, gw), lambda b, h, i: (b, 0, h)),
        ],
        out_specs=pl.BlockSpec((MOBA_BLOCK, gw), lambda b, h, i: (b * nb + i, h)),
        out_shape=jax.ShapeDtypeStruct(q16.shape, BF16),
        scratch_shapes=[pltpu.VMEM((group, MOBA_BLOCK, 1), F32), pltpu.VMEM((group, MOBA_BLOCK, 1), F32),
                        pltpu.VMEM((group, MOBA_BLOCK, HEAD_DIM), F32)],
        compiler_params=_params(("arbitrary", "arbitrary", "arbitrary")),
    )(slopes.reshape(n_heads // group, group, 1, 1), q16, k16, v16, kmean)


def _page_block_mean_kernel(pt_ref, *refs, bp):
    o_ref = refs[bp]
    tot = None
    for g in range(bp):
        part = jnp.sum(refs[g][...], axis=0)
        tot = part if tot is None else tot + part
    o_ref[...] = tot * (1.0 / (bp * PAGE_SIZE))


def _paged_block_means(cache_k, page_table, layer, n_full):
    db = page_table.shape[0]
    bp = MOBA_BLOCK // PAGE_SIZE
    n_heads, dh = cache_k.shape[3], cache_k.shape[4]

    def page_spec(g):
        return pl.BlockSpec((None, None, PAGE_SIZE, n_heads, dh),
                            lambda b, n, pt: (layer, pt[b, n * bp + g], 0, 0, 0))

    grid_spec = pltpu.PrefetchScalarGridSpec(
        num_scalar_prefetch=1,
        grid=(db, n_full),
        in_specs=[page_spec(g) for g in range(bp)],
        out_specs=pl.BlockSpec((None, None, n_heads, dh), lambda b, n, pt: (b, n, 0, 0)),
    )
    return pl.pallas_call(
        functools.partial(_page_block_mean_kernel, bp=bp),
        grid_spec=grid_spec,
        out_shape=jax.ShapeDtypeStruct((db, n_full, n_heads, dh), F32),
        compiler_params=_params(("arbitrary", "arbitrary")),
    )(page_table, *([cache_k] * bp))


def _moba_select_kernel(q_ref, km_ref, o_ref, *, n_heads, ksel):
    n_full = km_ref.shape[1]
    lane = lax.broadcasted_iota(I32, (q_ref.shape[0], 128), 1)
    for h in range(n_heads):
        c0 = h * HEAD_DIM
        gate = _dot_t(q_ref[:, c0:c0 + HEAD_DIM], km_ref[h].astype(BF16))
        picks = _top_blocks(gate, n_full, ksel)
        out = jnp.zeros(lane.shape, I32)
        for j, idx in enumerate(picks):
            out = jnp.where(lane == j, idx.astype(I32), out)
        o_ref[h] = out


def _moba_select(q16, kmean, db, ds, ksel):
    width = q16.shape[1]
    n_heads = width // HEAD_DIM
    n_full = kmean.shape[2]
    out = pl.pallas_call(
        functools.partial(_moba_select_kernel, n_heads=n_heads, ksel=ksel),
        grid=(db,),
        in_specs=[pl.BlockSpec((ds, width), lambda b: (b, 0)),
                  pl.BlockSpec((None, n_heads, n_full, HEAD_DIM), lambda b: (b, 0, 0, 0))],
        out_specs=pl.BlockSpec((None, n_heads, ds, 128), lambda b: (b, 0, 0, 0)),
        out_shape=jax.ShapeDtypeStruct((db, n_heads, ds, 128), I32),
        compiler_params=_params(("arbitrary",)),
    )(q16, kmean)
    return out[..., :ksel]


def _moba_sample_kernel(pt_ref, sel_ref, slopes_ref, q_ref, kn_ref, vn_ref, ck_hbm, cv_hbm, o_ref,
                        kbuf, vbuf, ksem, vsem, *, layer, ksel, bp, past):
    b = pl.program_id(0)
    h = pl.program_id(1)
    n_heads = pl.num_programs(1)
    t = b * n_heads + h
    slot = lax.rem(t, 2)
    ds = q_ref.shape[0]
    n_tiles = ksel * bp

    def copies(bb, hh, sl):
        out = []
        base = (bb * n_heads + hh) * ds * ksel
        for qn in range(ds):
            for j in range(ksel):
                blk = sel_ref[base + qn * ksel + j]
                for g in range(bp):
                    page = pt_ref[bb, blk * bp + g]
                    out.append(pltpu.make_async_copy(ck_hbm.at[layer, page, :, hh, :],
                                                     kbuf.at[sl, qn, j * bp + g], ksem.at[sl]))
                    out.append(pltpu.make_async_copy(cv_hbm.at[layer, page, :, hh, :],
                                                     vbuf.at[sl, qn, j * bp + g], vsem.at[sl]))
        return out

    @pl.when(t == 0)
    def _():
        for c in copies(b, h, slot):
            c.start()

    @pl.when(t + 1 < pl.num_programs(0) * n_heads)
    def _():
        wrap = h + 1 == n_heads
        for c in copies(jnp.where(wrap, b + 1, b), jnp.where(wrap, 0, h + 1), 1 - slot):
            c.start()

    for c in copies(b, h, slot):
        c.wait()

    slope = slopes_ref[h]
    q16 = q_ref[...]
    k3 = kbuf[slot].reshape(ds, n_tiles * PAGE_SIZE, HEAD_DIM).astype(BF16)
    v3 = vbuf[slot].reshape(ds, n_tiles * PAGE_SIZE, HEAD_DIM).astype(BF16)
    q3 = jnp.broadcast_to(q16[None], (ds, ds, HEAD_DIM))
    row = lax.broadcasted_iota(I32, (ds, PAGE_SIZE), 0)
    col = lax.broadcasted_iota(I32, (ds, PAGE_SIZE), 1)
    base = t * (ds * ksel)
    dist_rows = []
    for qn in range(ds):
        pieces = []
        for j in range(ksel):
            blk_start = sel_ref[base + qn * ksel + j] * MOBA_BLOCK
            for g in range(bp):
                pieces.append((row - col + (past - blk_start - g * PAGE_SIZE)).astype(F32))
        dist_rows.append(jnp.concatenate(pieces, axis=-1))
    dist3 = jnp.stack(dist_rows, axis=0)
    s = _bdot_t(q3, k3) - slope * dist3
    r8 = lax.broadcasted_iota(I32, (ds, ds), 0)
    c8 = lax.broadcasted_iota(I32, (ds, ds), 1)
    s_own = _dot_t(q16, kn_ref[...]) - slope * (r8 - c8).astype(F32)
    s_own = jnp.where(c8 <= r8, s_own, NEG_INF)
    s_own3 = jnp.broadcast_to(s_own[None], (ds, ds, ds))
    mx = jnp.maximum(jnp.max(s, axis=-1, keepdims=True), jnp.max(s_own3, axis=-1, keepdims=True))
    p = jnp.exp(s - mx)
    p_own = jnp.exp(s_own3 - mx)
    den = jnp.sum(p, axis=-1, keepdims=True) + jnp.sum(p_own, axis=-1, keepdims=True)
    out = _bdot((p / den).astype(BF16), v3)
    own = jnp.dot((p_own / den).reshape(ds * ds, ds).astype(BF16), vn_ref[...], preferred_element_type=F32)
    out = out + own.reshape(ds, ds, HEAD_DIM)
    keep = lax.broadcasted_iota(I32, (ds, HEAD_DIM), 0)
    res = jnp.zeros((ds, HEAD_DIM), F32)
    for qn in range(ds):
        res = res + jnp.where(keep == qn, out[qn], 0.0)
    o_ref[...] = res.astype(o_ref.dtype)


def _moba_sample(q16, k_new, v_new, cache_k, cache_v, page_table, sel, slopes, layer, db, ds):
    width = q16.shape[1]
    n_heads = width // HEAD_DIM
    ksel = sel.shape[-1]
    bp = MOBA_BLOCK // PAGE_SIZE
    n_pages = page_table.shape[1]
    n_tiles = ksel * bp
    head_spec = pl.BlockSpec((ds, HEAD_DIM), lambda b, h, pt, sl: (b, h))
    hbm = pl.BlockSpec(memory_space=pl.ANY)
    tile_buf = pltpu.VMEM((2, ds, n_tiles, PAGE_SIZE, HEAD_DIM), F32)
    grid_spec = pltpu.PrefetchScalarGridSpec(
        num_scalar_prefetch=2,
        grid=(db, n_heads),
        in_specs=[pl.BlockSpec(memory_space=pltpu.SMEM), head_spec, head_spec, head_spec, hbm, hbm],
        out_specs=head_spec,
        scratch_shapes=[tile_buf, tile_buf, pltpu.SemaphoreType.DMA((2,)), pltpu.SemaphoreType.DMA((2,))],
    )
    kern = functools.partial(_moba_sample_kernel, layer=layer, ksel=ksel, bp=bp, past=n_pages * PAGE_SIZE)
    return pl.pallas_call(
        kern,
        grid_spec=grid_spec,
        out_shape=jax.ShapeDtypeStruct(q16.shape, BF16),
        compiler_params=_params(("arbitrary", "arbitrary")),
    )(page_table, sel.reshape(-1), slopes, q16, k_new, v_new, cache_k, cache_v)


def _layer_norm_rows(z, g_ref, b_ref):
    mu = jnp.mean(z, axis=-1, keepdims=True)
    zc = z - mu
    var = jnp.mean(zc * zc, axis=-1, keepdims=True)
    return zc * lax.rsqrt(var + LN_EPS) * g_ref[...] + b_ref[...]


def _route(logits):
    lane = lax.broadcasted_iota(I32, logits.shape, 1)
    lane_f = lane.astype(F32)
    none = float(ROUTER_LANES)
    is_g = lane < MOE_GROUPS
    lg = jnp.where(is_g, logits, NEG_INF)
    mg = jnp.max(lg, axis=-1, keepdims=True)
    g_top = jnp.min(jnp.where(lg == mg, lane_f, none), axis=-1, keepdims=True)
    gate_g = 1.0 / jnp.sum(jnp.where(is_g, jnp.exp(logits - mg), 0.0), axis=-1, keepdims=True)
    e_lo = MOE_GROUPS + g_top * MOE_PER_GROUP
    le = jnp.where((lane_f >= e_lo) & (lane_f < e_lo + MOE_PER_GROUP), logits, NEG_INF)
    v1 = jnp.max(le, axis=-1, keepdims=True)
    i1 = jnp.min(jnp.where(le == v1, lane_f, none), axis=-1, keepdims=True)
    le2 = jnp.where(lane_f == i1, NEG_INF, le)
    v2 = jnp.max(le2, axis=-1, keepdims=True)
    i2 = jnp.min(jnp.where(le2 == v2, lane_f, none), axis=-1, keepdims=True)
    e2 = jnp.exp(v2 - v1)
    w1 = gate_g / (1.0 + e2)
    w2 = gate_g * e2 / (1.0 + e2)
    eid = jnp.where(lane == 0, i1 - MOE_GROUPS, jnp.where(lane == 1, i2 - MOE_GROUPS, 0.0)).astype(I32)
    wgt = jnp.where(lane == 0, w1, jnp.where(lane == 1, w2, 0.0))
    return eid, wgt


def _ln_router_kernel(z_ref, g_ref, b_ref, rw_ref, rb_ref, y32_ref, y16_ref, eid_ref, wgt_ref):
    y = _layer_norm_rows(z_ref[...], g_ref, b_ref)
    y32_ref[...] = y
    y16_ref[...] = y.astype(BF16)
    logits = jnp.dot(y.astype(BF16), rw_ref[...], preferred_element_type=F32) + rb_ref[...]
    eid, wgt = _route(logits)
    eid_ref[...] = eid
    wgt_ref[...] = wgt


def _ln_router(z, ln_g, ln_b, layer, which, rw, rb, *, tm=256):
    t, d = z.shape
    tm = min(tm, t)
    assert t % tm == 0
    row = pl.BlockSpec((tm, d), lambda i: (i, 0))
    vec = pl.BlockSpec((None, None, 1, d), lambda i: (layer, which, 0, 0))
    lane_blk = pl.BlockSpec((tm, ROUTER_LANES), lambda i: (i, 0))
    return pl.pallas_call(
        _ln_router_kernel,
        grid=(t // tm,),
        in_specs=[row, vec, vec, pl.BlockSpec((d, ROUTER_LANES), lambda i: (0, 0)),
                  pl.BlockSpec((1, ROUTER_LANES), lambda i: (0, 0))],
        out_specs=[row, row, lane_blk, lane_blk],
        out_shape=[jax.ShapeDtypeStruct((t, d), F32), jax.ShapeDtypeStruct((t, d), BF16),
                   jax.ShapeDtypeStruct((t, ROUTER_LANES), I32),
                   jax.ShapeDtypeStruct((t, ROUTER_LANES), F32)],
        compiler_params=_params(("arbitrary",)),
    )(z, ln_g.reshape(ln_g.shape[0], ln_g.shape[1], 1, d), ln_b.reshape(ln_b.shape[0], ln_b.shape[1], 1, d),
      rw, rb)


def _ln_combine_kernel(x_ref, y0_ref, y1_ref, w_ref, g_ref, b_ref, y32_ref, y16_ref, *, alpha):
    w = w_ref[...]
    z = alpha * x_ref[...] + (w[:, 0:1] * y0_ref[...] + w[:, 1:2] * y1_ref[...])
    y = _layer_norm_rows(z, g_ref, b_ref)
    y32_ref[...] = y
    y16_ref[...] = y.astype(BF16)


def _ln_combine(x, y0, y1, wgt, ln_g, ln_b, layer, which, alpha, *, tm=256):
    t, d = x.shape
    tm = min(tm, t)
    assert t % tm == 0
    row = pl.BlockSpec((tm, d), lambda i: (i, 0))
    vec = pl.BlockSpec((None, None, 1, d), lambda i: (layer, which, 0, 0))
    return pl.pallas_call(
        functools.partial(_ln_combine_kernel, alpha=alpha),
        grid=(t // tm,),
        in_specs=[row, row, row, pl.BlockSpec((tm, ROUTER_LANES), lambda i: (i, 0)), vec, vec],
        out_specs=[row, row],
        out_shape=[jax.ShapeDtypeStruct((t, d), F32), jax.ShapeDtypeStruct((t, d), BF16)],
        compiler_params=_params(("arbitrary",)),
    )(x, y0, y1, wgt, ln_g.reshape(ln_g.shape[0], ln_g.shape[1], 1, d),
      ln_b.reshape(ln_b.shape[0], ln_b.shape[1], 1, d))


def _moe_kernel(te_ref, nv_ref, x_ref, w1_ref, w3_ref, w2_ref, o_ref):
    t = pl.program_id(0)
    c = pl.program_id(1)

    @pl.when(t < nv_ref[0])
    def _():
        x = x_ref[...]
        a = jnp.dot(x, w1_ref[...].astype(BF16), preferred_element_type=F32)
        b = jnp.dot(x, w3_ref[...].astype(BF16), preferred_element_type=F32)
        hid = (a * jax.nn.sigmoid(a)) * b
        y = jnp.dot(hid.astype(BF16), w2_ref[...].astype(BF16), preferred_element_type=F32)

        @pl.when(c == 0)
        def _():
            o_ref[...] = y

        @pl.when(c != 0)
        def _():
            o_ref[...] += y

    @pl.when(t >= nv_ref[0])
    def _():
        o_ref[...] = jnp.zeros(o_ref.shape, o_ref.dtype)


def _moe_experts(xb, tile_expert, n_valid, w1, w3, w2, layer, *, n_chunks=2):
    n_slots, d = xb.shape
    n_tiles = n_slots // MOE_TILE
    d_exp = w1.shape[-1]
    ce = d_exp // n_chunks

    def chunk(t, c, nv):
        return jnp.where(t >= nv[0], 0, jnp.where(t % 2 == 0, c, n_chunks - 1 - c))

    grid_spec = pltpu.PrefetchScalarGridSpec(
        num_scalar_prefetch=2,
        grid=(n_tiles, n_chunks),
        in_specs=[
            pl.BlockSpec((MOE_TILE, d), lambda t, c, te, nv: (t, 0)),
            pl.BlockSpec((None, None, d, ce), lambda t, c, te, nv: (layer, te[t], 0, chunk(t, c, nv))),
            pl.BlockSpec((None, None, d, ce), lambda t, c, te, nv: (layer, te[t], 0, chunk(t, c, nv))),
            pl.BlockSpec((None, None, ce, d), lambda t, c, te, nv: (layer, te[t], chunk(t, c, nv), 0)),
        ],
        out_specs=pl.BlockSpec((MOE_TILE, d), lambda t, c, te, nv: (t, 0)),
    )
    return pl.pallas_call(
        _moe_kernel,
        grid_spec=grid_spec,
        out_shape=jax.ShapeDtypeStruct((n_slots, d), F32),
        compiler_params=_params(("arbitrary", "arbitrary")),
    )(tile_expert, n_valid, xb, w1, w3, w2)


def _moe_layer(x32, x16, eid, wgt, ln_g, ln_b, layer, w1, w3, w2, alpha):
    n_tok, d = x32.shape
    n_asg = n_tok * MOE_TOPK
    e_flat = eid[:, :MOE_TOPK].reshape(n_asg)
    onehot = (e_flat[:, None] == jnp.arange(N_EXPERTS, dtype=I32)[None, :]).astype(I32)
    csum = jnp.cumsum(onehot, axis=0)
    rank = jnp.sum(csum * onehot, axis=1) - 1
    counts = csum[-1]
    padded = (counts + MOE_TILE - 1) // MOE_TILE * MOE_TILE
    pad_ends = jnp.cumsum(padded)
    dest = (pad_ends - padded)[e_flat] + rank
    n_tiles = -(-(n_asg + N_EXPERTS * (MOE_TILE - 1)) // MOE_TILE)
    slot_tok = jnp.full((n_tiles * MOE_TILE,), n_tok, I32).at[dest].set(jnp.arange(n_asg, dtype=I32) // MOE_TOPK)
    tile_start = jnp.arange(n_tiles, dtype=I32) * MOE_TILE
    n_valid = (pad_ends[-1] // MOE_TILE).astype(I32)
    tile_expert = jnp.searchsorted(pad_ends, jnp.minimum(tile_start, pad_ends[-1] - 1), side="right")
    tile_expert = jnp.minimum(tile_expert, N_EXPERTS - 1).astype(I32)
    xb = jnp.take(x16, slot_tok, axis=0, mode="fill", fill_value=0)
    yb = _moe_experts(xb, tile_expert, n_valid.reshape(1), w1, w3, w2, layer)
    dest2 = dest.reshape(n_tok, MOE_TOPK)
    y0 = jnp.take(yb, dest2[:, 0], axis=0)
    y1 = jnp.take(yb, dest2[:, 1], axis=0)
    return _ln_combine(x32, y0, y1, wgt, ln_g, ln_b, layer, 1, alpha, tm=_row_tile(n_tok, 256))


def _router_weights(router_g_w, router_g_b, router_e_w, router_e_b, layer):
    d = router_g_w.shape[1]
    pad = ROUTER_LANES - MOE_GROUPS - N_EXPERTS
    rw = jnp.concatenate([router_g_w[layer], router_e_w[layer], jnp.zeros((d, pad), F32)], axis=1).astype(BF16)
    rb = jnp.concatenate([router_g_b[layer], router_e_b[layer], jnp.zeros((pad,), F32)])[None, :]
    return rw, rb


def kernel(x_prompt, x_sample, cache_k_diff, cache_v_diff, state_pool, cache_k_moba, cache_v_moba, page_table, w_in_even, pool_w, pool_scale, diff_lambda, diff_subln_g, w_out_even, w_in_odd, w_out_odd, ln_g, ln_b, router_g_w, router_g_b, router_e_w, router_e_b, moe_w1, moe_w3, moe_w2):
    bsz, seq, d = x_prompt.shape
    db, ds, _ = x_sample.shape
    depth = ln_g.shape[0]
    alpha = (2.0 * depth) ** 0.25
    n_pages = page_table.shape[1]
    past = n_pages * PAGE_SIZE
    tp, ts = bsz * seq, db * ds
    pool_width = pool_scale.shape[1]
    diff_width = cache_k_diff.shape[3] * cache_k_diff.shape[4]
    moba_width = cache_k_moba.shape[3] * cache_k_moba.shape[4]
    slopes_diff = jnp.asarray(_alibi_slopes(cache_k_diff.shape[3]))
    slopes_moba = jnp.asarray(_alibi_slopes(cache_k_moba.shape[3]))
    scale = HEAD_DIM ** -0.5
    tm_p = 512

    xp32, xs32 = x_prompt.reshape(tp, d), x_sample.reshape(ts, d)
    xp16, xs16 = xp32.astype(BF16), xs32.astype(BF16)
    kd_p, vd_p, pl_p, km_p, vm_p = [], [], [], [], []
    kd_s, vd_s, pl_s, km_s, vm_s = [], [], [], [], []
    for i in range(depth):
        j = i // 2
        if i % 2 == 0:
            lam_init = 0.8 - 0.6 * math.exp(-0.3 * i)
            zs = []
            for x16, x32, tm, is_prompt in ((xp16, xp32, tm_p, True), (xs16, xs32, ts, False)):
                t = x16.shape[0]
                (u32,) = _matmul([x16], w_in_even, j, 0, pool_width, [(F32, 1.0)], tm=tm)
                (q16,) = _matmul([x16], w_in_even, j, pool_width, diff_width, [(BF16, scale)], tm=tm)
                k32, k16 = _matmul([x16], w_in_even, j, pool_width + diff_width, diff_width,
                                   [(F32, 1.0), (BF16, 1.0)], tm=tm)
                v32, v16 = _matmul([x16], w_in_even, j, pool_width + 2 * diff_width, diff_width,
                                   [(F32, 1.0), (BF16, 1.0)], tm=tm)
                if is_prompt:
                    u3 = u32.reshape(bsz, seq, pool_width)
                    prefix = jnp.zeros((bsz, POOL_HALO, pool_width), F32)
                    pool = _pool_mix(u3, prefix, pool_w, pool_scale, j, 0, tl=256)
                    att = _diff_attn_prompt(q16, k16, v16, slopes_diff, diff_lambda, diff_subln_g, j,
                                            bsz, seq, lam_init)
                    ext = u3 if seq >= POOL_CARRY else jnp.concatenate([prefix[:, :POOL_CARRY], u3], axis=1)
                    kd_p.append(k32.reshape(bsz, seq, -1, 2 * HEAD_DIM))
                    vd_p.append(v32.reshape(bsz, seq, -1, 2 * HEAD_DIM))
                    pl_p.append(ext[:, -POOL_CARRY:])
                else:
                    u3 = u32.reshape(db, ds, pool_width)
                    prefix = jnp.concatenate([jnp.zeros((db, POOL_HALO - POOL_CARRY, pool_width), F32),
                                              state_pool[j]], axis=1)
                    pool = _pool_mix(u3, prefix, pool_w, pool_scale, j, past, tl=ds)
                    att = _diff_attn_sample(q16, k16, v16, cache_k_diff, cache_v_diff, page_table,
                                            slopes_diff, diff_lambda, diff_subln_g, j, db, ds, lam_init)
                    ext = jnp.concatenate([state_pool[j], u3], axis=1)
                    kd_s.append(k32.reshape(db, ds, -1, 2 * HEAD_DIM))
                    vd_s.append(v32.reshape(db, ds, -1, 2 * HEAD_DIM))
                    pl_s.append(ext[:, -POOL_CARRY:])
                (z,) = _matmul([pool, att], w_out_even, j, 0, d, [(F32, 1.0)], tm=tm, resid=x32, alpha=alpha)
                zs.append(z)
        else:
            zs = []
            for x16, x32, tm, is_prompt in ((xp16, xp32, tm_p, True), (xs16, xs32, ts, False)):
                (q16,) = _matmul([x16], w_in_odd, j, 0, moba_width, [(BF16, scale)], tm=tm)
                k32, k16 = _matmul([x16], w_in_odd, j, moba_width, moba_width, [(F32, 1.0), (BF16, 1.0)], tm=tm)
                v32, v16 = _matmul([x16], w_in_odd, j, 2 * moba_width, moba_width, [(F32, 1.0), (BF16, 1.0)], tm=tm)
                if is_prompt:
                    kmean = _block_means(k32, MOBA_BLOCK).reshape(bsz, seq // MOBA_BLOCK, moba_width)
                    att = _moba_prompt(q16, k16, v16, kmean, slopes_moba, bsz, seq)
                    km_p.append(k32.reshape(bsz, seq, -1, HEAD_DIM))
                    vm_p.append(v32.reshape(bsz, seq, -1, HEAD_DIM))
                else:
                    n_full = past // MOBA_BLOCK
                    assert past == n_full * MOBA_BLOCK and n_full >= 1
                    ksel = min(MOBA_TOPK, n_full)
                    kmean = _paged_block_means(cache_k_moba, page_table, j, n_full)
                    sel = _moba_select(q16, kmean.transpose(0, 2, 1, 3), db, ds, ksel)
                    att = _moba_sample(q16, k16, v16, cache_k_moba, cache_v_moba, page_table,
                                       sel, slopes_moba, j, db, ds)
                    km_s.append(k32.reshape(db, ds, -1, HEAD_DIM))
                    vm_s.append(v32.reshape(db, ds, -1, HEAD_DIM))
                w_out = w_out_odd
                (z,) = _matmul([att], w_out, j, 0, d, [(F32, 1.0)], tm=tm, resid=x32, alpha=alpha)
                zs.append(z)
        rw, rb = _router_weights(router_g_w, router_g_b, router_e_w, router_e_b, i)
        outs = [_ln_router(z, ln_g, ln_b, i, 0, rw, rb) for z in zs]
        x32 = jnp.concatenate([o[0] for o in outs], axis=0)
        x16 = jnp.concatenate([o[1] for o in outs], axis=0)
        eid = jnp.concatenate([o[2] for o in outs], axis=0)
        wgt = jnp.concatenate([o[3] for o in outs], axis=0)
        y32, y16 = _moe_layer(x32, x16, eid, wgt, ln_g, ln_b, i, moe_w1, moe_w3, moe_w2, alpha)
        xp32, xs32 = y32[:tp], y32[tp:]
        xp16, xs16 = y16[:tp], y16[tp:]
    return (xp32.reshape(bsz, seq, d), xs32.reshape(db, ds, d),
            jnp.stack(kd_p), jnp.stack(vd_p), jnp.stack(pl_p), jnp.stack(km_p), jnp.stack(vm_p),
            jnp.stack(kd_s), jnp.stack(vd_s), jnp.stack(pl_s), jnp.stack(km_s), jnp.stack(vm_s))
```

```python
import functools
import math

import numpy as np
import jax
import jax.numpy as jnp
from jax import lax
from jax.experimental import pallas as pl
from jax.experimental.pallas import tpu as pltpu

F32 = jnp.float32
BF16 = jnp.bfloat16
I32 = jnp.int32

HEAD_DIM = 128
POOL_WINDOWS = (2, 4, 8, 16)
POOL_CARRY = max(POOL_WINDOWS) - 1
POOL_HALO = 16
MOBA_BLOCK = 256
MOBA_TOPK = 3
PAGE_SIZE = 128
MOE_GROUPS = 4
MOE_PER_GROUP = 8
N_EXPERTS = MOE_GROUPS * MOE_PER_GROUP
MOE_TOPK = 2
LN_EPS = 1e-5
RMS_EPS = 1e-5
NEG_INF = float("-inf")

MOE_TILE = 256
ROUTER_LANES = 128
VMEM_LIMIT = 56 * 1024 * 1024


def _alibi_slopes(n):
    def pow2(m):
        start = 2.0 ** (-(2.0 ** -(math.log2(m) - 3)))
        return [start ** (i + 1) for i in range(m)]
    if (n & (n - 1)) == 0:
        s = pow2(n)
    else:
        c = 2 ** int(math.floor(math.log2(n)))
        s = pow2(c) + pow2(2 * c)[0::2][: n - c]
    return np.asarray(s, np.float32)


def _row_tile(n, cap):
    best = max(t for t in range(16, cap + 1, 16) if n % t == 0)
    return best


def _params(semantics):
    return pltpu.CompilerParams(dimension_semantics=semantics, vmem_limit_bytes=VMEM_LIMIT)


def _dot_t(a, b, precision=None):
    return lax.dot_general(a, b, (((1,), (1,)), ((), ())), preferred_element_type=F32,
                           precision=precision)


def _bdot_t(a, b):
    return lax.dot_general(a, b, (((2,), (2,)), ((0,), (0,))), preferred_element_type=F32)


def _bdot(a, b):
    return lax.dot_general(a, b, (((2,), (1,)), ((0,), (0,))), preferred_element_type=F32)


def _mm_kernel(*refs, k_splits, n_out, has_resid, alpha, scales, cast_rows):
    n_x = len(k_splits)
    x_refs = refs[:n_x]
    w_ref = refs[n_x]
    pos = n_x + 1
    resid_ref = refs[pos] if has_resid else None
    pos += int(has_resid)
    out_refs = refs[pos:pos + n_out]
    wbf_ref = refs[pos + n_out]

    @pl.when(pl.program_id(1) == 0)
    def _():
        def body(r, c):
            rows = pl.ds(pl.multiple_of(r * cast_rows, cast_rows), cast_rows)
            wbf_ref[rows, :] = w_ref[rows, :].astype(BF16)
            return c
        lax.fori_loop(0, w_ref.shape[0] // cast_rows, body, 0)

    acc = None
    for x_ref, (k0, k1) in zip(x_refs, k_splits):
        part = jnp.dot(x_ref[...], wbf_ref[k0:k1, :], preferred_element_type=F32)
        acc = part if acc is None else acc + part
    if has_resid:
        acc = alpha * resid_ref[...] + acc
    for o_ref, sc in zip(out_refs, scales):
        o_ref[...] = (acc if sc == 1.0 else acc * sc).astype(o_ref.dtype)


def _matmul(xs, w, layer, col0, ncols, outs, *, tm, tn=512, resid=None, alpha=1.0):
    m = xs[0].shape[0]
    k_total = w.shape[1]
    tm = min(tm, m)
    assert m % tm == 0 and ncols % tn == 0 and col0 % tn == 0
    k_splits, k0 = [], 0
    for x in xs:
        k_splits.append((k0, k0 + x.shape[1]))
        k0 += x.shape[1]
    assert k0 == k_total
    cb = col0 // tn
    in_specs = [pl.BlockSpec((tm, x.shape[1]), lambda j, i: (i, 0)) for x in xs]
    in_specs.append(pl.BlockSpec((None, k_total, tn), lambda j, i: (layer, 0, cb + j)))
    args = list(xs) + [w]
    if resid is not None:
        in_specs.append(pl.BlockSpec((tm, tn), lambda j, i: (i, j)))
        args.append(resid)
    kern = functools.partial(
        _mm_kernel, k_splits=tuple(k_splits), n_out=len(outs), has_resid=resid is not None,
        alpha=alpha, scales=tuple(s for _, s in outs), cast_rows=min(512, k_total))
    res = pl.pallas_call(
        kern,
        grid=(ncols // tn, m // tm),
        in_specs=in_specs,
        out_specs=[pl.BlockSpec((tm, tn), lambda j, i: (i, j)) for _ in outs],
        out_shape=[jax.ShapeDtypeStruct((m, ncols), dt) for dt, _ in outs],
        scratch_shapes=[pltpu.VMEM((k_total, tn), BF16)],
        compiler_params=_params(("arbitrary", "arbitrary")),
    )(*args)
    return res


def _pool_kernel(u_ref, pre_ref, w_ref, sc_ref, o_ref, ext_ref, *, tl, pos0, group):
    li = pl.program_id(1)

    @pl.when(li == 0)
    def _():
        ext_ref[0:POOL_HALO, :] = pre_ref[...]

    ext_ref[POOL_HALO:POOL_HALO + tl, :] = u_ref[...]
    pos = pos0 + li * tl + lax.broadcasted_iota(I32, (tl, 1), 0)
    for g, win in enumerate(POOL_WINDOWS):
        c0, c1 = g * group, (g + 1) * group
        cur = ext_ref[POOL_HALO:POOL_HALO + tl, c0:c1]
        tot = cur
        for i in range(1, win):
            tot = tot + ext_ref[POOL_HALO - i:POOL_HALO - i + tl, c0:c1]
        cnt = jnp.minimum(win, pos + 1).astype(F32)
        pooled = tot / cnt - cur
        y = jnp.dot(pooled.astype(BF16), w_ref[g].astype(BF16), preferred_element_type=F32)
        o_ref[:, c0:c1] = (y * sc_ref[:, c0:c1]).astype(o_ref.dtype)
    if tl >= POOL_HALO:
        ext_ref[0:POOL_HALO, :] = ext_ref[tl:tl + POOL_HALO, :]


def _pool_mix(u, prefix, pool_w, pool_scale, layer, pos0, *, tl):
    bsz, length, width = u.shape
    tl = min(tl, length)
    assert length % tl == 0 and (length == tl or tl >= POOL_HALO)
    group = width // len(POOL_WINDOWS)
    kern = functools.partial(_pool_kernel, tl=tl, pos0=pos0, group=group)
    out = pl.pallas_call(
        kern,
        grid=(bsz, length // tl),
        in_specs=[
            pl.BlockSpec((None, tl, width), lambda b, l: (b, l, 0)),
            pl.BlockSpec((None, POOL_HALO, width), lambda b, l: (b, 0, 0)),
            pl.BlockSpec((None, len(POOL_WINDOWS), group, group), lambda b, l: (layer, 0, 0, 0)),
            pl.BlockSpec((None, 1, width), lambda b, l: (layer, 0, 0)),
        ],
        out_specs=pl.BlockSpec((None, tl, width), lambda b, l: (b, l, 0)),
        out_shape=jax.ShapeDtypeStruct((bsz, length, width), BF16),
        scratch_shapes=[pltpu.VMEM((POOL_HALO + tl, width), F32)],
        compiler_params=_params(("arbitrary", "arbitrary")),
    )(u, prefix, pool_w, pool_scale.reshape(pool_scale.shape[0], 1, width))
    return out.reshape(bsz * length, width)


def _diff_combine(o1, o2, lam_ref, g_ref, lam_init):
    lam = lam_ref[...]
    e1 = jnp.exp(jnp.sum(lam[0:1] * lam[1:2], axis=-1, keepdims=True))
    e2 = jnp.exp(jnp.sum(lam[2:3] * lam[3:4], axis=-1, keepdims=True))
    lam_full = e1 - e2 + lam_init
    d = o1 - lam_full * o2
    ms = jnp.mean(d * d, axis=-1, keepdims=True)
    return d * lax.rsqrt(ms + RMS_EPS) * g_ref[...] * (1.0 - lam_init)


def _diff_prompt_kernel(slopes_ref, q_ref, k_ref, v_ref, lam_ref, g_ref, o_ref, m_ref, l_ref, acc_ref, *,
                        tq, lam_init, group):
    qi = pl.program_id(2)
    dv = 2 * HEAD_DIM
    nmap = 2 * group
    slope3 = slopes_ref[...]
    row = lax.broadcasted_iota(I32, (tq, tq), 0)
    col = lax.broadcasted_iota(I32, (tq, tq), 1)
    rel = slope3 * (col - row).astype(F32)[None]
    q3 = _heads(q_ref, slice(None), nmap, HEAD_DIM)

    def values(rows):
        return jnp.stack([v_ref[rows, (j // 2) * dv:(j // 2 + 1) * dv] for j in range(nmap)], axis=0)

    d0 = pl.ds(pl.multiple_of(qi * tq, tq), tq)
    s = _bdot_t(q3, _heads(k_ref, d0, nmap, HEAD_DIM)) + rel
    s = jnp.where((col <= row)[None], s, NEG_INF)
    m0 = jnp.max(s, axis=-1, keepdims=True)
    p = jnp.exp(s - m0)
    m_ref[...] = m0
    l_ref[...] = jnp.sum(p, axis=-1, keepdims=True)
    acc_ref[...] = _bdot(p.astype(BF16), values(d0))

    def body(kj, c):
        k0 = pl.ds(pl.multiple_of(kj * tq, tq), tq)
        s = _bdot_t(q3, _heads(k_ref, k0, nmap, HEAD_DIM)) + (rel - slope3 * ((qi - kj) * tq).astype(F32))
        m_prev = m_ref[...]
        m_new = jnp.maximum(m_prev, jnp.max(s, axis=-1, keepdims=True))
        corr = jnp.exp(m_prev - m_new)
        p = jnp.exp(s - m_new)
        l_ref[...] = corr * l_ref[...] + jnp.sum(p, axis=-1, keepdims=True)
        acc_ref[...] = corr * acc_ref[...] + _bdot(p.astype(BF16), values(k0))
        m_ref[...] = m_new
        return c

    lax.fori_loop(0, qi, body, 0)
    o = acc_ref[...] / l_ref[...]
    for i in range(group):
        o_ref[:, i * dv:(i + 1) * dv] = _diff_combine(o[2 * i], o[2 * i + 1], lam_ref, g_ref,
                                                       lam_init).astype(o_ref.dtype)


def _diff_attn_prompt(q, k, v, slopes, diff_lambda, subln_g, layer, bsz, seq, lam_init, *, tq=256, group=2):
    dv = 2 * HEAD_DIM
    n_heads = q.shape[1] // dv
    tq = min(tq, seq)
    assert seq % tq == 0 and n_heads % group == 0
    nq = seq // tq
    gw = group * dv
    kern = functools.partial(_diff_prompt_kernel, tq=tq, lam_init=lam_init, group=group)
    return pl.pallas_call(
        kern,
        grid=(bsz, n_heads // group, nq),
        in_specs=[
            pl.BlockSpec((None, 2 * group, 1, 1), lambda b, h, i: (h, 0, 0, 0)),
            pl.BlockSpec((tq, gw), lambda b, h, i: (b * nq + i, h)),
            pl.BlockSpec((seq, gw), lambda b, h, i: (b, h)),
            pl.BlockSpec((seq, gw), lambda b, h, i: (b, h)),
            pl.BlockSpec((None, 4, HEAD_DIM), lambda b, h, i: (layer, 0, 0)),
            pl.BlockSpec((None, 1, dv), lambda b, h, i: (layer, 0, 0)),
        ],
        out_specs=pl.BlockSpec((tq, gw), lambda b, h, i: (b * nq + i, h)),
        out_shape=jax.ShapeDtypeStruct(q.shape, BF16),
        scratch_shapes=[pltpu.VMEM((2 * group, tq, 1), F32), pltpu.VMEM((2 * group, tq, 1), F32),
                        pltpu.VMEM((2 * group, tq, dv), F32)],
        compiler_params=_params(("arbitrary", "arbitrary", "arbitrary")),
    )(jnp.repeat(slopes, 2).reshape(n_heads // group, 2 * group, 1, 1), q, k, v, diff_lambda,
      subln_g.reshape(subln_g.shape[0], 1, dv))


def _diff_sample_kernel(pt_ref, slopes_ref, q_ref, kn_ref, vn_ref, *rest, pages_per_step, past, lam_init):
    pg = pages_per_step
    ck_refs = rest[:pg]
    cv_refs = rest[pg:2 * pg]
    lam_ref, g_ref, o_ref, m_ref, l_ref, acc_ref = rest[2 * pg:]
    step = pl.program_id(1)
    n_steps = pl.num_programs(1)
    ds = q_ref.shape[2]

    @pl.when(step == 0)
    def _():
        m_ref[...] = jnp.full(m_ref.shape, NEG_INF, F32)
        l_ref[...] = jnp.zeros(l_ref.shape, F32)
        acc_ref[...] = jnp.zeros(acc_ref.shape, F32)

    slope3 = slopes_ref[...]
    row = lax.broadcasted_iota(I32, (ds, PAGE_SIZE), 0)
    col = lax.broadcasted_iota(I32, (ds, PAGE_SIZE), 1)
    dist0 = row - col + (past - step * (pg * PAGE_SIZE))
    qs = (q_ref[0], q_ref[1])
    m_run = [m_ref[0], m_ref[1]]
    l_run = [l_ref[0], l_ref[1]]
    acc = [acc_ref[0], acc_ref[1]]
    for g in range(pg):
        bias = slope3 * (dist0 - g * PAGE_SIZE).astype(F32)[None]
        k3 = ck_refs[g][...].astype(BF16)
        v3 = cv_refs[g][...].astype(BF16)
        for mi in range(2):
            s = _bdot_t(qs[mi], k3[:, :, mi * HEAD_DIM:(mi + 1) * HEAD_DIM]) - bias
            m_new = jnp.maximum(m_run[mi], jnp.max(s, axis=-1, keepdims=True))
            corr = jnp.exp(m_run[mi] - m_new)
            p = jnp.exp(s - m_new)
            l_run[mi] = l_run[mi] * corr + jnp.sum(p, axis=-1, keepdims=True)
            acc[mi] = acc[mi] * corr + _bdot(p.astype(BF16), v3)
            m_run[mi] = m_new
    for mi in range(2):
        m_ref[mi], l_ref[mi], acc_ref[mi] = m_run[mi], l_run[mi], acc[mi]

    @pl.when(step == n_steps - 1)
    def _():
        r8 = lax.broadcasted_iota(I32, (ds, ds), 0)
        c8 = lax.broadcasted_iota(I32, (ds, ds), 1)
        bias8 = slope3 * (r8 - c8).astype(F32)[None]
        outs = []
        for mi in range(2):
            s = _bdot_t(qs[mi], kn_ref[mi]) - bias8
            s = jnp.where((c8 <= r8)[None], s, NEG_INF)
            m_new = jnp.maximum(m_run[mi], jnp.max(s, axis=-1, keepdims=True))
            corr = jnp.exp(m_run[mi] - m_new)
            p = jnp.exp(s - m_new)
            l_fin = l_run[mi] * corr + jnp.sum(p, axis=-1, keepdims=True)
            a_fin = acc[mi] * corr + _bdot(p.astype(BF16), vn_ref[...])
            outs.append(a_fin / l_fin)
        o_ref[...] = _diff_combine(outs[0], outs[1], lam_ref, g_ref, lam_init).astype(o_ref.dtype)


def _diff_attn_sample(q, k_new, v_new, cache_k, cache_v, page_table, slopes, diff_lambda, subln_g,
                      layer, db, ds, lam_init, *, pages_per_step=4):
    dv = 2 * HEAD_DIM
    width = q.shape[1]
    n_heads = width // dv
    n_pages = page_table.shape[1]
    pg = math.gcd(pages_per_step, n_pages)
    q5 = q.reshape(db, ds, n_heads, 2, HEAD_DIM).transpose(0, 3, 2, 1, 4)
    kn5 = k_new.reshape(db, ds, n_heads, 2, HEAD_DIM).transpose(0, 3, 2, 1, 4)
    vn4 = v_new.reshape(db, ds, n_heads, dv).transpose(0, 2, 1, 3)
    map_spec = pl.BlockSpec((None, 2, n_heads, ds, HEAD_DIM), lambda b, s, pt: (b, 0, 0, 0, 0))
    head_spec = pl.BlockSpec((None, n_heads, ds, dv), lambda b, s, pt: (b, 0, 0, 0))
    ck = cache_k.transpose(0, 1, 3, 2, 4)
    cv = cache_v.transpose(0, 1, 3, 2, 4)

    def page_spec(g):
        return pl.BlockSpec((None, None, n_heads, PAGE_SIZE, dv),
                            lambda b, s, pt: (layer, pt[b, s * pg + g], 0, 0, 0))

    kern = functools.partial(_diff_sample_kernel, pages_per_step=pg, past=n_pages * PAGE_SIZE,
                             lam_init=lam_init)
    grid_spec = pltpu.PrefetchScalarGridSpec(
        num_scalar_prefetch=1,
        grid=(db, n_pages // pg),
        in_specs=[pl.BlockSpec((n_heads, 1, 1), lambda b, s, pt: (0, 0, 0)), map_spec, map_spec, head_spec]
        + [page_spec(g) for g in range(pg)] + [page_spec(g) for g in range(pg)]
        + [pl.BlockSpec((None, 4, HEAD_DIM), lambda b, s, pt: (layer, 0, 0)),
           pl.BlockSpec((None, 1, dv), lambda b, s, pt: (layer, 0, 0))],
        out_specs=head_spec,
        scratch_shapes=[pltpu.VMEM((2, n_heads, ds, 1), F32), pltpu.VMEM((2, n_heads, ds, 1), F32),
                        pltpu.VMEM((2, n_heads, ds, dv), F32)],
    )
    out = pl.pallas_call(
        kern,
        grid_spec=grid_spec,
        out_shape=jax.ShapeDtypeStruct((db, n_heads, ds, dv), BF16),
        compiler_params=_params(("arbitrary", "arbitrary")),
    )(page_table, slopes.reshape(n_heads, 1, 1), q5, kn5, vn4, *([ck] * pg), *([cv] * pg), diff_lambda,
      subln_g.reshape(subln_g.shape[0], 1, dv))
    return out.transpose(0, 2, 1, 3).reshape(db * ds, width)


def _top_blocks(gate, n_valid, ksel):
    nb = gate.shape[-1]
    lane = lax.broadcasted_iota(I32, gate.shape, gate.ndim - 1)
    lane_f = lane.astype(F32)
    valid = lane < n_valid
    g = jnp.where(valid, gate, NEG_INF)
    picks = []
    for _ in range(ksel):
        mx = jnp.max(g, axis=-1, keepdims=True)
        idx = jnp.min(jnp.where((g == mx) & valid, lane_f, float(nb)), axis=-1, keepdims=True)
        picks.append(idx)
        hit = lane_f == idx
        g = jnp.where(hit, NEG_INF, g)
        valid = valid & jnp.logical_not(hit)
    return picks


def _block_mean_kernel(k_ref, o_ref):
    o_ref[...] = jnp.mean(k_ref[...], axis=0, keepdims=True)


def _block_means(k32, rows):
    t, width = k32.shape
    out = pl.pallas_call(
        _block_mean_kernel,
        grid=(t // rows,),
        in_specs=[pl.BlockSpec((rows, width), lambda i: (i, 0))],
        out_specs=pl.BlockSpec((None, 1, width), lambda i: (i, 0, 0)),
        out_shape=jax.ShapeDtypeStruct((t // rows, 1, width), F32),
        compiler_params=_params(("arbitrary",)),
    )(k32)
    return out.reshape(t // rows, width)


def _heads(ref, rows, n, width):
    return jnp.stack([ref[rows, i * width:(i + 1) * width] for i in range(n)], axis=0)


def _moba_prompt_kernel(slopes_ref, q_ref, k_ref, v_ref, km_ref, o_ref, m_ref, l_ref, acc_ref, *, ksel,
                        group):
    qi = pl.program_id(2)
    blk = MOBA_BLOCK
    slope3 = slopes_ref[...]
    q3 = _heads(q_ref, slice(None), group, HEAD_DIM)
    row = lax.broadcasted_iota(I32, (blk, blk), 0)
    col = lax.broadcasted_iota(I32, (blk, blk), 1)
    rel = slope3 * (col - row).astype(F32)[None]

    d0 = pl.ds(pl.multiple_of(qi * blk, blk), blk)
    s = _bdot_t(q3, _heads(k_ref, d0, group, HEAD_DIM)) + rel
    s = jnp.where((col <= row)[None], s, NEG_INF)
    m0 = jnp.max(s, axis=-1, keepdims=True)
    p = jnp.exp(s - m0)
    m_ref[...] = m0
    l_ref[...] = jnp.sum(p, axis=-1, keepdims=True)
    acc_ref[...] = _bdot(p.astype(BF16), _heads(v_ref, d0, group, HEAD_DIM))

    gate = _bdot_t(q3, _heads(km_ref, slice(None), group, HEAD_DIM).astype(BF16))
    picks = _top_blocks(gate, qi, ksel)

    def body(kj, c):
        k0 = pl.ds(pl.multiple_of(kj * blk, blk), blk)
        kj_f = kj.astype(F32)
        chosen = picks[0] == kj_f
        for idx in picks[1:]:
            chosen = chosen | (idx == kj_f)
        row_bias = jnp.where(chosen, 0.0, NEG_INF) - slope3 * ((qi - kj) * blk).astype(F32)
        s = _bdot_t(q3, _heads(k_ref, k0, group, HEAD_DIM)) + rel + row_bias
        m_prev = m_ref[...]
        m_new = jnp.maximum(m_prev, jnp.max(s, axis=-1, keepdims=True))
        corr = jnp.exp(m_prev - m_new)
        p = jnp.exp(s - m_new)
        l_ref[...] = corr * l_ref[...] + jnp.sum(p, axis=-1, keepdims=True)
        acc_ref[...] = corr * acc_ref[...] + _bdot(p.astype(BF16), _heads(v_ref, k0, group, HEAD_DIM))
        m_ref[...] = m_new
        return c

    lax.fori_loop(0, qi, body, 0)
    out = acc_ref[...] / l_ref[...]
    for i in range(group):
        o_ref[:, i * HEAD_DIM:(i + 1) * HEAD_DIM] = out[i].astype(o_ref.dtype)


def _moba_prompt(q16, k16, v16, kmean, slopes, bsz, seq, *, group=4):
    width = q16.shape[1]
    n_heads = width // HEAD_DIM
    assert seq % MOBA_BLOCK == 0 and n_heads % group == 0
    nb = seq // MOBA_BLOCK
    gw = group * HEAD_DIM
    ksel = min(MOBA_TOPK, nb - 1)
    if ksel == 0:
        ksel = 1
    kern = functools.partial(_moba_prompt_kernel, ksel=ksel, group=group)
    return pl.pallas_call(
        kern,
        grid=(bsz, n_heads // group, nb),
        in_specs=[
            pl.BlockSpec((None, group, 1, 1), lambda b, h, i: (h, 0, 0, 0)),
            pl.BlockSpec((MOBA_BLOCK, gw), lambda b, h, i: (b * nb + i, h)),
            pl.BlockSpec((seq, gw), lambda b, h, i: (b, h)),
            pl.BlockSpec((seq, gw), lambda b, h, i: (b, h)),
            pl.BlockSpec((None, nb, gw), lambda b, h, i: (b, 0, h)),
        ],
        out_specs=pl.BlockSpec((MOBA_BLOCK, gw), lambda b, h, i: (b * nb + i, h)),
        out_shape=jax.ShapeDtypeStruct(q16.shape, BF16),
        scratch_shapes=[pltpu.VMEM((group, MOBA_BLOCK, 1), F32), pltpu.VMEM((group, MOBA_BLOCK, 1), F32),
                        pltpu.VMEM((group, MOBA_BLOCK, HEAD_DIM), F32)],
        compiler_params=_params(("arbitrary", "arbitrary", "arbitrary")),
    )(slopes.reshape(n_heads // group, group, 1, 1), q16, k16, v16, kmean)


def _page_block_mean_kernel(pt_ref, *refs, bp):
    o_ref = refs[bp]
    tot = None
    for g in range(bp):
        part = jnp.sum(refs[g][...], axis=0)
        tot = part if tot is None else tot + part
    o_ref[...] = tot * (1.0 / (bp * PAGE_SIZE))


def _paged_block_means(cache_k, page_table, layer, n_full):
    db = page_table.shape[0]
    bp = MOBA_BLOCK // PAGE_SIZE
    n_heads, dh = cache_k.shape[3], cache_k.shape[4]

    def page_spec(g):
        return pl.BlockSpec((None, None, PAGE_SIZE, n_heads, dh),
                            lambda b, n, pt: (layer, pt[b, n * bp + g], 0, 0, 0))

    grid_spec = pltpu.PrefetchScalarGridSpec(
        num_scalar_prefetch=1,
        grid=(db, n_full),
        in_specs=[page_spec(g) for g in range(bp)],
        out_specs=pl.BlockSpec((None, None, n_heads, dh), lambda b, n, pt: (b, n, 0, 0)),
    )
    return pl.pallas_call(
        functools.partial(_page_block_mean_kernel, bp=bp),
        grid_spec=grid_spec,
        out_shape=jax.ShapeDtypeStruct((db, n_full, n_heads, dh), F32),
        compiler_params=_params(("arbitrary", "arbitrary")),
    )(page_table, *([cache_k] * bp))


def _moba_select_kernel(q_ref, km_ref, o_ref, *, n_heads, ksel):
    n_full = km_ref.shape[1]
    lane = lax.broadcasted_iota(I32, (q_ref.shape[0], 128), 1)
    for h in range(n_heads):
        c0 = h * HEAD_DIM
        gate = _dot_t(q_ref[:, c0:c0 + HEAD_DIM], km_ref[h].astype(BF16))
        picks = _top_blocks(gate, n_full, ksel)
        out = jnp.zeros(lane.shape, I32)
        for j, idx in enumerate(picks):
            out = jnp.where(lane == j, idx.astype(I32), out)
        o_ref[h] = out


def _moba_select(q16, kmean, db, ds, ksel):
    width = q16.shape[1]
    n_heads = width // HEAD_DIM
    n_full = kmean.shape[2]
    out = pl.pallas_call(
        functools.partial(_moba_select_kernel, n_heads=n_heads, ksel=ksel),
        grid=(db,),
        in_specs=[pl.BlockSpec((ds, width), lambda b: (b, 0)),
                  pl.BlockSpec((None, n_heads, n_full, HEAD_DIM), lambda b: (b, 0, 0, 0))],
        out_specs=pl.BlockSpec((None, n_heads, ds, 128), lambda b: (b, 0, 0, 0)),
        out_shape=jax.ShapeDtypeStruct((db, n_heads, ds, 128), I32),
        compiler_params=_params(("arbitrary",)),
    )(q16, kmean)
    return out[..., :ksel]


def _moba_sample_kernel(pt_ref, sel_ref, slopes_ref, q_ref, kn_ref, vn_ref, ck_hbm, cv_hbm, o_ref,
                        kbuf, vbuf, ksem, vsem, *, layer, ksel, bp, past):
    b = pl.program_id(0)
    h = pl.program_id(1)
    n_heads = pl.num_programs(1)
    t = b * n_heads + h
    slot = lax.rem(t, 2)
    ds = q_ref.shape[0]
    n_tiles = ksel * bp

    def copies(bb, hh, sl):
        out = []
        base = (bb * n_heads + hh) * ds * ksel
        for qn in range(ds):
            for j in range(ksel):
                blk = sel_ref[base + qn * ksel + j]
                for g in range(bp):
                    page = pt_ref[bb, blk * bp + g]
                    out.append(pltpu.make_async_copy(ck_hbm.at[layer, page, :, hh, :],
                                                     kbuf.at[sl, qn, j * bp + g], ksem.at[sl]))
                    out.append(pltpu.make_async_copy(cv_hbm.at[layer, page, :, hh, :],
                                                     vbuf.at[sl, qn, j * bp + g], vsem.at[sl]))
        return out

    @pl.when(t == 0)
    def _():
        for c in copies(b, h, slot):
            c.start()

    @pl.when(t + 1 < pl.num_programs(0) * n_heads)
    def _():
        wrap = h + 1 == n_heads
        for c in copies(jnp.where(wrap, b + 1, b), jnp.where(wrap, 0, h + 1), 1 - slot):
            c.start()

    for c in copies(b, h, slot):
        c.wait()

    slope = slopes_ref[h]
    q16 = q_ref[...]
    k3 = kbuf[slot].reshape(ds, n_tiles * PAGE_SIZE, HEAD_DIM).astype(BF16)
    v3 = vbuf[slot].reshape(ds, n_tiles * PAGE_SIZE, HEAD_DIM).astype(BF16)
    q3 = jnp.broadcast_to(q16[None], (ds, ds, HEAD_DIM))
    row = lax.broadcasted_iota(I32, (ds, PAGE_SIZE), 0)
    col = lax.broadcasted_iota(I32, (ds, PAGE_SIZE), 1)
    base = t * (ds * ksel)
    dist_rows = []
    for qn in range(ds):
        pieces = []
        for j in range(ksel):
            blk_start = sel_ref[base + qn * ksel + j] * MOBA_BLOCK
            for g in range(bp):
                pieces.append((row - col + (past - blk_start - g * PAGE_SIZE)).astype(F32))
        dist_rows.append(jnp.concatenate(pieces, axis=-1))
    dist3 = jnp.stack(dist_rows, axis=0)
    s = _bdot_t(q3, k3) - slope * dist3
    r8 = lax.broadcasted_iota(I32, (ds, ds), 0)
    c8 = lax.broadcasted_iota(I32, (ds, ds), 1)
    s_own = _dot_t(q16, kn_ref[...]) - slope * (r8 - c8).astype(F32)
    s_own = jnp.where(c8 <= r8, s_own, NEG_INF)
    s_own3 = jnp.broadcast_to(s_own[None], (ds, ds, ds))
    mx = jnp.maximum(jnp.max(s, axis=-1, keepdims=True), jnp.max(s_own3, axis=-1, keepdims=True))
    p = jnp.exp(s - mx)
    p_own = jnp.exp(s_own3 - mx)
    den = jnp.sum(p, axis=-1, keepdims=True) + jnp.sum(p_own, axis=-1, keepdims=True)
    out = _bdot((p / den).astype(BF16), v3)
    own = jnp.dot((p_own / den).reshape(ds * ds, ds).astype(BF16), vn_ref[...], preferred_element_type=F32)
    out = out + own.reshape(ds, ds, HEAD_DIM)
    keep = lax.broadcasted_iota(I32, (ds, HEAD_DIM), 0)
    res = jnp.zeros((ds, HEAD_DIM), F32)
    for qn in range(ds):
        res = res + jnp.where(keep == qn, out[qn], 0.0)
    o_ref[...] = res.astype(o_ref.dtype)


def _moba_sample(q16, k_new, v_new, cache_k, cache_v, page_table, sel, slopes, layer, db, ds):
    width = q16.shape[1]
    n_heads = width // HEAD_DIM
    ksel = sel.shape[-1]
    bp = MOBA_BLOCK // PAGE_SIZE
    n_pages = page_table.shape[1]
    n_tiles = ksel * bp
    head_spec = pl.BlockSpec((ds, HEAD_DIM), lambda b, h, pt, sl: (b, h))
    hbm = pl.BlockSpec(memory_space=pl.ANY)
    tile_buf = pltpu.VMEM((2, ds, n_tiles, PAGE_SIZE, HEAD_DIM), F32)
    grid_spec = pltpu.PrefetchScalarGridSpec(
        num_scalar_prefetch=2,
        grid=(db, n_heads),
        in_specs=[pl.BlockSpec(memory_space=pltpu.SMEM), head_spec, head_spec, head_spec, hbm, hbm],
        out_specs=head_spec,
        scratch_shapes=[tile_buf, tile_buf, pltpu.SemaphoreType.DMA((2,)), pltpu.SemaphoreType.DMA((2,))],
    )
    kern = functools.partial(_moba_sample_kernel, layer=layer, ksel=ksel, bp=bp, past=n_pages * PAGE_SIZE)
    return pl.pallas_call(
        kern,
        grid_spec=grid_spec,
        out_shape=jax.ShapeDtypeStruct(q16.shape, BF16),
        compiler_params=_params(("arbitrary", "arbitrary")),
    )(page_table, sel.reshape(-1), slopes, q16, k_new, v_new, cache_k, cache_v)


def _layer_norm_rows(z, g_ref, b_ref):
    mu = jnp.mean(z, axis=-1, keepdims=True)
    zc = z - mu
    var = jnp.mean(zc * zc, axis=-1, keepdims=True)
    return zc * lax.rsqrt(var + LN_EPS) * g_ref[...] + b_ref[...]


def _route(logits):
    lane = lax.broadcasted_iota(I32, logits.shape, 1)
    lane_f = lane.astype(F32)
    none = float(ROUTER_LANES)
    is_g = lane < MOE_GROUPS
    lg = jnp.where(is_g, logits, NEG_INF)
    mg = jnp.max(lg, axis=-1, keepdims=True)
    g_top = jnp.min(jnp.where(lg == mg, lane_f, none), axis=-1, keepdims=True)
    gate_g = 1.0 / jnp.sum(jnp.where(is_g, jnp.exp(logits - mg), 0.0), axis=-1, keepdims=True)
    e_lo = MOE_GROUPS + g_top * MOE_PER_GROUP
    le = jnp.where((lane_f >= e_lo) & (lane_f < e_lo + MOE_PER_GROUP), logits, NEG_INF)
    v1 = jnp.max(le, axis=-1, keepdims=True)
    i1 = jnp.min(jnp.where(le == v1, lane_f, none), axis=-1, keepdims=True)
    le2 = jnp.where(lane_f == i1, NEG_INF, le)
    v2 = jnp.max(le2, axis=-1, keepdims=True)
    i2 = jnp.min(jnp.where(le2 == v2, lane_f, none), axis=-1, keepdims=True)
    e2 = jnp.exp(v2 - v1)
    w1 = gate_g / (1.0 + e2)
    w2 = gate_g * e2 / (1.0 + e2)
    eid = jnp.where(lane == 0, i1 - MOE_GROUPS, jnp.where(lane == 1, i2 - MOE_GROUPS, 0.0)).astype(I32)
    wgt = jnp.where(lane == 0, w1, jnp.where(lane == 1, w2, 0.0))
    return eid, wgt


def _ln_router_kernel(z_ref, g_ref, b_ref, rw_ref, rb_ref, y32_ref, eid_ref, wgt_ref):
    y = _layer_norm_rows(z_ref[...], g_ref, b_ref)
    y32_ref[...] = y
    logits = jnp.dot(y.astype(BF16), rw_ref[...], preferred_element_type=F32) + rb_ref[...]
    eid, wgt = _route(logits)
    eid_ref[...] = eid
    wgt_ref[...] = wgt


def _ln_router(z, ln_g, ln_b, layer, which, rw, rb, *, tm=256):
    t, d = z.shape
    tm = min(tm, t)
    assert t % tm == 0
    row = pl.BlockSpec((tm, d), lambda i: (i, 0))
    vec = pl.BlockSpec((None, None, 1, d), lambda i: (layer, which, 0, 0))
    lane_blk = pl.BlockSpec((tm, ROUTER_LANES), lambda i: (i, 0))
    return pl.pallas_call(
        _ln_router_kernel,
        grid=(t // tm,),
        in_specs=[row, vec, vec, pl.BlockSpec((d, ROUTER_LANES), lambda i: (0, 0)),
                  pl.BlockSpec((1, ROUTER_LANES), lambda i: (0, 0))],
        out_specs=[row, lane_blk, lane_blk],
        out_shape=[jax.ShapeDtypeStruct((t, d), F32),
                   jax.ShapeDtypeStruct((t, ROUTER_LANES), I32),
                   jax.ShapeDtypeStruct((t, ROUTER_LANES), F32)],
        compiler_params=_params(("arbitrary",)),
    )(z, ln_g.reshape(ln_g.shape[0], ln_g.shape[1], 1, d), ln_b.reshape(ln_b.shape[0], ln_b.shape[1], 1, d),
      rw, rb)


def _ln_combine_kernel(x_ref, y0_ref, y1_ref, w_ref, g_ref, b_ref, y32_ref, y16_ref, *, alpha):
    w = w_ref[...]
    z = alpha * x_ref[...] + (w[:, 0:1] * y0_ref[...] + w[:, 1:2] * y1_ref[...])
    y = _layer_norm_rows(z, g_ref, b_ref)
    y32_ref[...] = y
    y16_ref[...] = y.astype(BF16)


def _ln_combine(x, y, wgt, ln_g, ln_b, layer, which, alpha, *, tm=256):
    t, d = x.shape
    tm = min(tm, t)
    assert t % tm == 0
    row = pl.BlockSpec((tm, d), lambda i: (i, 0))
    vec = pl.BlockSpec((None, None, 1, d), lambda i: (layer, which, 0, 0))
    return pl.pallas_call(
        functools.partial(_ln_combine_kernel, alpha=alpha),
        grid=(t // tm,),
        in_specs=[row, pl.BlockSpec((None, tm, d), lambda i: (0, i, 0)),
                  pl.BlockSpec((None, tm, d), lambda i: (1, i, 0)),
                  pl.BlockSpec((tm, ROUTER_LANES), lambda i: (i, 0)), vec, vec],
        out_specs=[row, row],
        out_shape=[jax.ShapeDtypeStruct((t, d), F32), jax.ShapeDtypeStruct((t, d), BF16)],
        compiler_params=_params(("arbitrary",)),
    )(x, y, y, wgt, ln_g.reshape(ln_g.shape[0], ln_g.shape[1], 1, d),
      ln_b.reshape(ln_b.shape[0], ln_b.shape[1], 1, d))


def _moe_kernel(te_ref, nv_ref, tab_ref, x_hbm, w1_ref, w3_ref, w2_ref, out_hbm, xbuf, obuf, gsem, ssem, *,
                n_asg):
    t = pl.program_id(0)
    c = pl.program_id(1)
    n_tiles = pl.num_programs(0)
    n_chunks = pl.num_programs(1)
    nv = nv_ref[0]
    sl = lax.rem(t, 2)
    n_tok = n_asg // MOE_TOPK

    def slot_asg(tile, r):
        idx = tile * MOE_TILE + r
        word = tab_ref[lax.shift_right_logical(idx, 1)]
        return jnp.where((idx & 1) == 0, word & 0xFFFF, lax.shift_right_logical(word, 16))

    def gather(tile, buf, start):
        def body(r, carry):
            tok = lax.shift_right_logical(jnp.minimum(slot_asg(tile, r), n_asg - 1), 1)
            cp = pltpu.make_async_copy(x_hbm.at[pl.ds(tok, 1)], xbuf.at[buf, pl.ds(r, 1)], gsem.at[buf])
            if start:
                cp.start()
            else:
                cp.wait()
            return carry
        lax.fori_loop(0, MOE_TILE, body, 0)

    def scatter(tile, buf, start):
        def body(r, carry):
            a = slot_asg(tile, r)

            @pl.when(a < n_asg)
            def _():
                row = (a & 1) * n_tok + lax.shift_right_logical(a, 1)
                cp = pltpu.make_async_copy(obuf.at[buf, pl.ds(r, 1)], out_hbm.at[pl.ds(row, 1)], ssem.at[buf])
                if start:
                    cp.start()
                else:
                    cp.wait()
            return carry
        lax.fori_loop(0, MOE_TILE, body, 0)

    @pl.when(c == 0)
    def _():
        @pl.when(t == 0)
        def _():
            gather(t, sl, True)

        @pl.when(t < nv)
        def _():
            gather(t, sl, False)

        @pl.when(t + 1 < nv)
        def _():
            gather(t + 1, 1 - sl, True)

        @pl.when((t >= 2) & (t - 2 < nv))
        def _():
            scatter(t - 2, sl, False)

    @pl.when(t < nv)
    def _():
        x = xbuf[sl].astype(BF16)
        a = jnp.dot(x, w1_ref[...].astype(BF16), preferred_element_type=F32)
        b = jnp.dot(x, w3_ref[...].astype(BF16), preferred_element_type=F32)
        hid = (a * jax.nn.sigmoid(a)) * b
        y = jnp.dot(hid.astype(BF16), w2_ref[...].astype(BF16), preferred_element_type=F32)

        @pl.when(c == 0)
        def _():
            obuf[sl] = y

        @pl.when(c != 0)
        def _():
            obuf[sl] += y

        @pl.when(c == n_chunks - 1)
        def _():
            scatter(t, sl, True)

    @pl.when((t == n_tiles - 1) & (c == n_chunks - 1))
    def _():
        @pl.when((t >= 1) & (t - 1 < nv))
        def _():
            scatter(t - 1, 1 - sl, False)

        @pl.when(t < nv)
        def _():
            scatter(t, sl, False)


def _moe_experts(x32, slot_table, tile_expert, n_valid, w1, w3, w2, layer, n_tiles, *, n_chunks=2):
    n_tok, d = x32.shape
    n_asg = n_tok * MOE_TOPK
    d_exp = w1.shape[-1]
    ce = d_exp // n_chunks

    def chunk(t, c, nv):
        return jnp.where(t >= nv[0], 0, jnp.where(t % 2 == 0, c, n_chunks - 1 - c))

    grid_spec = pltpu.PrefetchScalarGridSpec(
        num_scalar_prefetch=3,
        grid=(n_tiles, n_chunks),
        in_specs=[
            pl.BlockSpec(memory_space=pl.ANY),
            pl.BlockSpec((None, None, d, ce), lambda t, c, te, nv, tab: (layer, te[t], 0, chunk(t, c, nv))),
            pl.BlockSpec((None, None, d, ce), lambda t, c, te, nv, tab: (layer, te[t], 0, chunk(t, c, nv))),
            pl.BlockSpec((None, None, ce, d), lambda t, c, te, nv, tab: (layer, te[t], chunk(t, c, nv), 0)),
        ],
        out_specs=pl.BlockSpec(memory_space=pl.ANY),
        scratch_shapes=[pltpu.VMEM((2, MOE_TILE, d), F32), pltpu.VMEM((2, MOE_TILE, d), F32),
                        pltpu.SemaphoreType.DMA((2,)), pltpu.SemaphoreType.DMA((2,))],
    )
    return pl.pallas_call(
        functools.partial(_moe_kernel, n_asg=n_asg),
        grid_spec=grid_spec,
        out_shape=jax.ShapeDtypeStruct((n_asg, d), F32),
        compiler_params=_params(("arbitrary", "arbitrary")),
    )(tile_expert, n_valid, slot_table, x32, w1, w3, w2)


def _moe_layer(x32, eid, wgt, ln_g, ln_b, layer, w1, w3, w2, alpha):
    n_tok, d = x32.shape
    n_asg = n_tok * MOE_TOPK
    assert n_asg < (1 << 16) and MOE_TOPK == 2
    e_flat = eid[:, :MOE_TOPK].reshape(n_asg)
    onehot = (e_flat[:, None] == jnp.arange(N_EXPERTS, dtype=I32)[None, :]).astype(I32)
    csum = jnp.cumsum(onehot, axis=0)
    rank = jnp.sum(csum * onehot, axis=1) - 1
    counts = csum[-1]
    padded = (counts + MOE_TILE - 1) // MOE_TILE * MOE_TILE
    pad_ends = jnp.cumsum(padded)
    dest = (pad_ends - padded)[e_flat] + rank
    n_tiles = -(-(n_asg + N_EXPERTS * (MOE_TILE - 1)) // MOE_TILE)
    slot_asg = jnp.full((n_tiles * MOE_TILE,), n_asg, I32).at[dest].set(jnp.arange(n_asg, dtype=I32))
    slot_table = slot_asg[0::2] | (slot_asg[1::2] << 16)
    tile_start = jnp.arange(n_tiles, dtype=I32) * MOE_TILE
    n_valid = (pad_ends[-1] // MOE_TILE).astype(I32)
    tile_expert = jnp.searchsorted(pad_ends, jnp.minimum(tile_start, pad_ends[-1] - 1), side="right")
    tile_expert = jnp.minimum(tile_expert, N_EXPERTS - 1).astype(I32)
    y = _moe_experts(x32, slot_table, tile_expert, n_valid.reshape(1), w1, w3, w2, layer, n_tiles)
    y = y.reshape(MOE_TOPK, n_tok, d)
    return _ln_combine(x32, y, wgt, ln_g, ln_b, layer, 1, alpha, tm=_row_tile(n_tok, 256))


def _router_weights(router_g_w, router_g_b, router_e_w, router_e_b, layer):
    d = router_g_w.shape[1]
    pad = ROUTER_LANES - MOE_GROUPS - N_EXPERTS
    rw = jnp.concatenate([router_g_w[layer], router_e_w[layer], jnp.zeros((d, pad), F32)], axis=1).astype(BF16)
    rb = jnp.concatenate([router_g_b[layer], router_e_b[layer], jnp.zeros((pad,), F32)])[None, :]
    return rw, rb


def kernel(x_prompt, x_sample, cache_k_diff, cache_v_diff, state_pool, cache_k_moba, cache_v_moba, page_table, w_in_even, pool_w, pool_scale, diff_lambda, diff_subln_g, w_out_even, w_in_odd, w_out_odd, ln_g, ln_b, router_g_w, router_g_b, router_e_w, router_e_b, moe_w1, moe_w3, moe_w2):
    bsz, seq, d = x_prompt.shape
    db, ds, _ = x_sample.shape
    depth = ln_g.shape[0]
    alpha = (2.0 * depth) ** 0.25
    n_pages = page_table.shape[1]
    past = n_pages * PAGE_SIZE
    tp, ts = bsz * seq, db * ds
    pool_width = pool_scale.shape[1]
    diff_width = cache_k_diff.shape[3] * cache_k_diff.shape[4]
    moba_width = cache_k_moba.shape[3] * cache_k_moba.shape[4]
    slopes_diff = jnp.asarray(_alibi_slopes(cache_k_diff.shape[3]))
    slopes_moba = jnp.asarray(_alibi_slopes(cache_k_moba.shape[3]))
    scale = HEAD_DIM ** -0.5
    tm_p = 512

    xp32, xs32 = x_prompt.reshape(tp, d), x_sample.reshape(ts, d)
    xp16, xs16 = xp32.astype(BF16), xs32.astype(BF16)
    kd_p, vd_p, pl_p, km_p, vm_p = [], [], [], [], []
    kd_s, vd_s, pl_s, km_s, vm_s = [], [], [], [], []
    for i in range(depth):
        j = i // 2
        if i % 2 == 0:
            lam_init = 0.8 - 0.6 * math.exp(-0.3 * i)
            zs = []
            for x16, x32, tm, is_prompt in ((xp16, xp32, tm_p, True), (xs16, xs32, ts, False)):
                t = x16.shape[0]
                (u32,) = _matmul([x16], w_in_even, j, 0, pool_width, [(F32, 1.0)], tm=tm)
                (q16,) = _matmul([x16], w_in_even, j, pool_width, diff_width, [(BF16, scale)], tm=tm)
                k32, k16 = _matmul([x16], w_in_even, j, pool_width + diff_width, diff_width,
                                   [(F32, 1.0), (BF16, 1.0)], tm=tm)
                v32, v16 = _matmul([x16], w_in_even, j, pool_width + 2 * diff_width, diff_width,
                                   [(F32, 1.0), (BF16, 1.0)], tm=tm)
                if is_prompt:
                    u3 = u32.reshape(bsz, seq, pool_width)
                    prefix = jnp.zeros((bsz, POOL_HALO, pool_width), F32)
                    pool = _pool_mix(u3, prefix, pool_w, pool_scale, j, 0, tl=256)
                    att = _diff_attn_prompt(q16, k16, v16, slopes_diff, diff_lambda, diff_subln_g, j,
                                            bsz, seq, lam_init)
                    ext = u3 if seq >= POOL_CARRY else jnp.concatenate([prefix[:, :POOL_CARRY], u3], axis=1)
                    kd_p.append(k32.reshape(bsz, seq, -1, 2 * HEAD_DIM))
                    vd_p.append(v32.reshape(bsz, seq, -1, 2 * HEAD_DIM))
                    pl_p.append(ext[:, -POOL_CARRY:])
                else:
                    u3 = u32.reshape(db, ds, pool_width)
                    prefix = jnp.concatenate([jnp.zeros((db, POOL_HALO - POOL_CARRY, pool_width), F32),
                                              state_pool[j]], axis=1)
                    pool = _pool_mix(u3, prefix, pool_w, pool_scale, j, past, tl=ds)
                    att = _diff_attn_sample(q16, k16, v16, cache_k_diff, cache_v_diff, page_table,
                                            slopes_diff, diff_lambda, diff_subln_g, j, db, ds, lam_init)
                    ext = jnp.concatenate([state_pool[j], u3], axis=1)
                    kd_s.append(k32.reshape(db, ds, -1, 2 * HEAD_DIM))
                    vd_s.append(v32.reshape(db, ds, -1, 2 * HEAD_DIM))
                    pl_s.append(ext[:, -POOL_CARRY:])
                (z,) = _matmul([pool, att], w_out_even, j, 0, d, [(F32, 1.0)], tm=tm, resid=x32, alpha=alpha)
                zs.append(z)
        else:
            zs = []
            for x16, x32, tm, is_prompt in ((xp16, xp32, tm_p, True), (xs16, xs32, ts, False)):
                (q16,) = _matmul([x16], w_in_odd, j, 0, moba_width, [(BF16, scale)], tm=tm)
                k32, k16 = _matmul([x16], w_in_odd, j, moba_width, moba_width, [(F32, 1.0), (BF16, 1.0)], tm=tm)
                v32, v16 = _matmul([x16], w_in_odd, j, 2 * moba_width, moba_width, [(F32, 1.0), (BF16, 1.0)], tm=tm)
                if is_prompt:
                    kmean = _block_means(k32, MOBA_BLOCK).reshape(bsz, seq // MOBA_BLOCK, moba_width)
                    att = _moba_prompt(q16, k16, v16, kmean, slopes_moba, bsz, seq)
                    km_p.append(k32.reshape(bsz, seq, -1, HEAD_DIM))
                    vm_p.append(v32.reshape(bsz, seq, -1, HEAD_DIM))
                else:
                    n_full = past // MOBA_BLOCK
                    assert past == n_full * MOBA_BLOCK and n_full >= 1
                    ksel = min(MOBA_TOPK, n_full)
                    kmean = _paged_block_means(cache_k_moba, page_table, j, n_full)
                    sel = _moba_select(q16, kmean.transpose(0, 2, 1, 3), db, ds, ksel)
                    att = _moba_sample(q16, k16, v16, cache_k_moba, cache_v_moba, page_table,
                                       sel, slopes_moba, j, db, ds)
                    km_s.append(k32.reshape(db, ds, -1, HEAD_DIM))
                    vm_s.append(v32.reshape(db, ds, -1, HEAD_DIM))
                w_out = w_out_odd
                (z,) = _matmul([att], w_out, j, 0, d, [(F32, 1.0)], tm=tm, resid=x32, alpha=alpha)
                zs.append(z)
        rw, rb = _router_weights(router_g_w, router_g_b, router_e_w, router_e_b, i)
        outs = [_ln_router(z, ln_g, ln_b, i, 0, rw, rb) for z in zs]
        x32 = jnp.concatenate([o[0] for o in outs], axis=0)
        eid = jnp.concatenate([o[1] for o in outs], axis=0)
        wgt = jnp.concatenate([o[2] for o in outs], axis=0)
        y32, y16 = _moe_layer(x32, eid, wgt, ln_g, ln_b, i, moe_w1, moe_w3, moe_w2, alpha)
        xp32, xs32 = y32[:tp], y32[tp:]
        xp16, xs16 = y16[:tp], y16[tp:]
    return (xp32.reshape(bsz, seq, d), xs32.reshape(db, ds, d),
            jnp.stack(kd_p), jnp.stack(vd_p), jnp.stack(pl_p), jnp.stack(km_p), jnp.stack(vm_p),
            jnp.stack(kd_s), jnp.stack(vd_s), jnp.stack(pl_s), jnp.stack(km_s), jnp.stack(vm_s))
```

```python
import functools
import math

import numpy as np
import jax
import jax.numpy as jnp
from jax import lax
from jax.experimental import pallas as pl
from jax.experimental.pallas import tpu as pltpu

F32 = jnp.float32
BF16 = jnp.bfloat16
I32 = jnp.int32

HEAD_DIM = 128
POOL_WINDOWS = (2, 4, 8, 16)
POOL_CARRY = max(POOL_WINDOWS) - 1
POOL_HALO = 16
MOBA_BLOCK = 256
MOBA_TOPK = 3
PAGE_SIZE = 128
MOE_GROUPS = 4
MOE_PER_GROUP = 8
N_EXPERTS = MOE_GROUPS * MOE_PER_GROUP
MOE_TOPK = 2
LN_EPS = 1e-5
RMS_EPS = 1e-5
NEG_INF = float("-inf")

MOE_TILE = 256
ROUTER_LANES = 128
VMEM_LIMIT = 56 * 1024 * 1024


def _alibi_slopes(n):
    def pow2(m):
        start = 2.0 ** (-(2.0 ** -(math.log2(m) - 3)))
        return [start ** (i + 1) for i in range(m)]
    if (n & (n - 1)) == 0:
        s = pow2(n)
    else:
        c = 2 ** int(math.floor(math.log2(n)))
        s = pow2(c) + pow2(2 * c)[0::2][: n - c]
    return np.asarray(s, np.float32)


def _row_tile(n, cap):
    best = max(t for t in range(16, cap + 1, 16) if n % t == 0)
    return best


def _params(semantics):
    return pltpu.CompilerParams(dimension_semantics=semantics, vmem_limit_bytes=VMEM_LIMIT)


def _dot_t(a, b, precision=None):
    return lax.dot_general(a, b, (((1,), (1,)), ((), ())), preferred_element_type=F32,
                           precision=precision)


def _bdot_t(a, b):
    return lax.dot_general(a, b, (((2,), (2,)), ((0,), (0,))), preferred_element_type=F32)


def _bdot(a, b):
    return lax.dot_general(a, b, (((2,), (1,)), ((0,), (0,))), preferred_element_type=F32)


def _mm_kernel(*refs, k_splits, n_out, has_resid, alpha, scales, cast_rows):
    n_x = len(k_splits)
    x_refs = refs[:n_x]
    w_ref = refs[n_x]
    pos = n_x + 1
    resid_ref = refs[pos] if has_resid else None
    pos += int(has_resid)
    out_refs = refs[pos:pos + n_out]
    wbf_ref = refs[pos + n_out]

    @pl.when(pl.program_id(1) == 0)
    def _():
        def body(r, c):
            rows = pl.ds(pl.multiple_of(r * cast_rows, cast_rows), cast_rows)
            wbf_ref[rows, :] = w_ref[rows, :].astype(BF16)
            return c
        lax.fori_loop(0, w_ref.shape[0] // cast_rows, body, 0)

    acc = None
    for x_ref, (k0, k1) in zip(x_refs, k_splits):
        part = jnp.dot(x_ref[...], wbf_ref[k0:k1, :], preferred_element_type=F32)
        acc = part if acc is None else acc + part
    if has_resid:
        acc = alpha * resid_ref[...] + acc
    for o_ref, sc in zip(out_refs, scales):
        o_ref[...] = (acc if sc == 1.0 else acc * sc).astype(o_ref.dtype)


def _matmul(xs, w, layer, col0, ncols, outs, *, tm, tn=512, resid=None, alpha=1.0, m=None, x_row0=0,
            resid_row0=0):
    m = xs[0].shape[0] if m is None else m
    k_total = w.shape[1]
    tm = min(tm, m)
    assert m % tm == 0 and ncols % tn == 0 and col0 % tn == 0
    assert x_row0 % tm == 0 and resid_row0 % tm == 0
    xb0, rb0 = x_row0 // tm, resid_row0 // tm
    k_splits, k0 = [], 0
    for x in xs:
        k_splits.append((k0, k0 + x.shape[1]))
        k0 += x.shape[1]
    assert k0 == k_total
    cb = col0 // tn
    in_specs = [pl.BlockSpec((tm, x.shape[1]), lambda j, i: (xb0 + i, 0)) for x in xs]
    in_specs.append(pl.BlockSpec((None, k_total, tn), lambda j, i: (layer, 0, cb + j)))
    args = list(xs) + [w]
    if resid is not None:
        in_specs.append(pl.BlockSpec((tm, tn), lambda j, i: (rb0 + i, j)))
        args.append(resid)
    kern = functools.partial(
        _mm_kernel, k_splits=tuple(k_splits), n_out=len(outs), has_resid=resid is not None,
        alpha=alpha, scales=tuple(s for _, s in outs), cast_rows=min(512, k_total))
    res = pl.pallas_call(
        kern,
        grid=(ncols // tn, m // tm),
        in_specs=in_specs,
        out_specs=[pl.BlockSpec((tm, tn), lambda j, i: (i, j)) for _ in outs],
        out_shape=[jax.ShapeDtypeStruct((m, ncols), dt) for dt, _ in outs],
        scratch_shapes=[pltpu.VMEM((k_total, tn), BF16)],
        compiler_params=_params(("arbitrary", "arbitrary")),
    )(*args)
    return res


def _pool_kernel(u_ref, pre_ref, w_ref, sc_ref, o_ref, ext_ref, *, tl, pos0, group):
    li = pl.program_id(1)

    @pl.when(li == 0)
    def _():
        ext_ref[0:POOL_HALO, :] = pre_ref[...]

    ext_ref[POOL_HALO:POOL_HALO + tl, :] = u_ref[...]
    pos = pos0 + li * tl + lax.broadcasted_iota(I32, (tl, 1), 0)
    for g, win in enumerate(POOL_WINDOWS):
        c0, c1 = g * group, (g + 1) * group
        cur = ext_ref[POOL_HALO:POOL_HALO + tl, c0:c1]
        tot = cur
        for i in range(1, win):
            tot = tot + ext_ref[POOL_HALO - i:POOL_HALO - i + tl, c0:c1]
        cnt = jnp.minimum(win, pos + 1).astype(F32)
        pooled = tot / cnt - cur
        y = jnp.dot(pooled.astype(BF16), w_ref[g].astype(BF16), preferred_element_type=F32)
        o_ref[:, c0:c1] = (y * sc_ref[:, c0:c1]).astype(o_ref.dtype)
    if tl >= POOL_HALO:
        ext_ref[0:POOL_HALO, :] = ext_ref[tl:tl + POOL_HALO, :]


def _pool_mix(u, prefix, pool_w, pool_scale, layer, pos0, *, tl):
    bsz, length, width = u.shape
    tl = min(tl, length)
    assert length % tl == 0 and (length == tl or tl >= POOL_HALO)
    group = width // len(POOL_WINDOWS)
    kern = functools.partial(_pool_kernel, tl=tl, pos0=pos0, group=group)
    out = pl.pallas_call(
        kern,
        grid=(bsz, length // tl),
        in_specs=[
            pl.BlockSpec((None, tl, width), lambda b, l: (b, l, 0)),
            pl.BlockSpec((None, POOL_HALO, width), lambda b, l: (b, 0, 0)),
            pl.BlockSpec((None, len(POOL_WINDOWS), group, group), lambda b, l: (layer, 0, 0, 0)),
            pl.BlockSpec((None, 1, width), lambda b, l: (layer, 0, 0)),
        ],
        out_specs=pl.BlockSpec((None, tl, width), lambda b, l: (b, l, 0)),
        out_shape=jax.ShapeDtypeStruct((bsz, length, width), BF16),
        scratch_shapes=[pltpu.VMEM((POOL_HALO + tl, width), F32)],
        compiler_params=_params(("arbitrary", "arbitrary")),
    )(u, prefix, pool_w, pool_scale.reshape(pool_scale.shape[0], 1, width))
    return out.reshape(bsz * length, width)


def _diff_combine(o1, o2, lam_ref, g_ref, lam_init):
    lam = lam_ref[...]
    e1 = jnp.exp(jnp.sum(lam[0:1] * lam[1:2], axis=-1, keepdims=True))
    e2 = jnp.exp(jnp.sum(lam[2:3] * lam[3:4], axis=-1, keepdims=True))
    lam_full = e1 - e2 + lam_init
    d = o1 - lam_full * o2
    ms = jnp.mean(d * d, axis=-1, keepdims=True)
    return d * lax.rsqrt(ms + RMS_EPS) * g_ref[...] * (1.0 - lam_init)


def _diff_prompt_kernel(slopes_ref, q_ref, k_ref, v_ref, lam_ref, g_ref, o_ref, m_ref, l_ref, acc_ref, *,
                        tq, lam_init, group):
    qi = pl.program_id(2)
    dv = 2 * HEAD_DIM
    nmap = 2 * group
    slope3 = slopes_ref[...]
    row = lax.broadcasted_iota(I32, (tq, tq), 0)
    col = lax.broadcasted_iota(I32, (tq, tq), 1)
    rel = slope3 * (col - row).astype(F32)[None]
    q3 = _heads(q_ref, slice(None), nmap, HEAD_DIM)

    def values(rows):
        return jnp.stack([v_ref[rows, (j // 2) * dv:(j // 2 + 1) * dv] for j in range(nmap)], axis=0)

    d0 = pl.ds(pl.multiple_of(qi * tq, tq), tq)
    s = _bdot_t(q3, _heads(k_ref, d0, nmap, HEAD_DIM)) + rel
    s = jnp.where((col <= row)[None], s, NEG_INF)
    m0 = jnp.max(s, axis=-1, keepdims=True)
    p = jnp.exp(s - m0)
    m_ref[...] = m0
    l_ref[...] = jnp.sum(p, axis=-1, keepdims=True)
    acc_ref[...] = _bdot(p.astype(BF16), values(d0))

    def body(kj, c):
        k0 = pl.ds(pl.multiple_of(kj * tq, tq), tq)
        s = _bdot_t(q3, _heads(k_ref, k0, nmap, HEAD_DIM)) + (rel - slope3 * ((qi - kj) * tq).astype(F32))
        m_prev = m_ref[...]
        m_new = jnp.maximum(m_prev, jnp.max(s, axis=-1, keepdims=True))
        corr = jnp.exp(m_prev - m_new)
        p = jnp.exp(s - m_new)
        l_ref[...] = corr * l_ref[...] + jnp.sum(p, axis=-1, keepdims=True)
        acc_ref[...] = corr * acc_ref[...] + _bdot(p.astype(BF16), values(k0))
        m_ref[...] = m_new
        return c

    lax.fori_loop(0, qi, body, 0)
    o = acc_ref[...] / l_ref[...]
    for i in range(group):
        o_ref[:, i * dv:(i + 1) * dv] = _diff_combine(o[2 * i], o[2 * i + 1], lam_ref, g_ref,
                                                       lam_init).astype(o_ref.dtype)


def _diff_attn_prompt(q, k, v, slopes, diff_lambda, subln_g, layer, bsz, seq, lam_init, *, tq=256, group=2):
    dv = 2 * HEAD_DIM
    n_heads = q.shape[1] // dv
    tq = min(tq, seq)
    assert seq % tq == 0 and n_heads % group == 0
    nq = seq // tq
    gw = group * dv
    kern = functools.partial(_diff_prompt_kernel, tq=tq, lam_init=lam_init, group=group)
    return pl.pallas_call(
        kern,
        grid=(bsz, n_heads // group, nq),
        in_specs=[
            pl.BlockSpec((None, 2 * group, 1, 1), lambda b, h, i: (h, 0, 0, 0)),
            pl.BlockSpec((tq, gw), lambda b, h, i: (b * nq + i, h)),
            pl.BlockSpec((seq, gw), lambda b, h, i: (b, h)),
            pl.BlockSpec((seq, gw), lambda b, h, i: (b, h)),
            pl.BlockSpec((None, 4, HEAD_DIM), lambda b, h, i: (layer, 0, 0)),
            pl.BlockSpec((None, 1, dv), lambda b, h, i: (layer, 0, 0)),
        ],
        out_specs=pl.BlockSpec((tq, gw), lambda b, h, i: (b * nq + i, h)),
        out_shape=jax.ShapeDtypeStruct(q.shape, BF16),
        scratch_shapes=[pltpu.VMEM((2 * group, tq, 1), F32), pltpu.VMEM((2 * group, tq, 1), F32),
                        pltpu.VMEM((2 * group, tq, dv), F32)],
        compiler_params=_params(("arbitrary", "arbitrary", "arbitrary")),
    )(jnp.repeat(slopes, 2).reshape(n_heads // group, 2 * group, 1, 1), q, k, v, diff_lambda,
      subln_g.reshape(subln_g.shape[0], 1, dv))


def _diff_sample_kernel(pt_ref, slopes_ref, q_ref, kn_ref, vn_ref, *rest, pages_per_step, past, lam_init):
    pg = pages_per_step
    ck_refs = rest[:pg]
    cv_refs = rest[pg:2 * pg]
    lam_ref, g_ref, o_ref, m_ref, l_ref, acc_ref = rest[2 * pg:]
    step = pl.program_id(1)
    n_steps = pl.num_programs(1)
    ds = q_ref.shape[2]

    @pl.when(step == 0)
    def _():
        m_ref[...] = jnp.full(m_ref.shape, NEG_INF, F32)
        l_ref[...] = jnp.zeros(l_ref.shape, F32)
        acc_ref[...] = jnp.zeros(acc_ref.shape, F32)

    slope3 = slopes_ref[...]
    row = lax.broadcasted_iota(I32, (ds, PAGE_SIZE), 0)
    col = lax.broadcasted_iota(I32, (ds, PAGE_SIZE), 1)
    dist0 = row - col + (past - step * (pg * PAGE_SIZE))
    qs = (q_ref[0], q_ref[1])
    m_run = [m_ref[0], m_ref[1]]
    l_run = [l_ref[0], l_ref[1]]
    acc = [acc_ref[0], acc_ref[1]]
    for g in range(pg):
        bias = slope3 * (dist0 - g * PAGE_SIZE).astype(F32)[None]
        k3 = ck_refs[g][...].astype(BF16)
        v3 = cv_refs[g][...].astype(BF16)
        for mi in range(2):
            s = _bdot_t(qs[mi], k3[:, :, mi * HEAD_DIM:(mi + 1) * HEAD_DIM]) - bias
            m_new = jnp.maximum(m_run[mi], jnp.max(s, axis=-1, keepdims=True))
            corr = jnp.exp(m_run[mi] - m_new)
            p = jnp.exp(s - m_new)
            l_run[mi] = l_run[mi] * corr + jnp.sum(p, axis=-1, keepdims=True)
            acc[mi] = acc[mi] * corr + _bdot(p.astype(BF16), v3)
            m_run[mi] = m_new
    for mi in range(2):
        m_ref[mi], l_ref[mi], acc_ref[mi] = m_run[mi], l_run[mi], acc[mi]

    @pl.when(step == n_steps - 1)
    def _():
        r8 = lax.broadcasted_iota(I32, (ds, ds), 0)
        c8 = lax.broadcasted_iota(I32, (ds, ds), 1)
        bias8 = slope3 * (r8 - c8).astype(F32)[None]
        outs = []
        for mi in range(2):
            s = _bdot_t(qs[mi], kn_ref[mi]) - bias8
            s = jnp.where((c8 <= r8)[None], s, NEG_INF)
            m_new = jnp.maximum(m_run[mi], jnp.max(s, axis=-1, keepdims=True))
            corr = jnp.exp(m_run[mi] - m_new)
            p = jnp.exp(s - m_new)
            l_fin = l_run[mi] * corr + jnp.sum(p, axis=-1, keepdims=True)
            a_fin = acc[mi] * corr + _bdot(p.astype(BF16), vn_ref[...])
            outs.append(a_fin / l_fin)
        o_ref[...] = _diff_combine(outs[0], outs[1], lam_ref, g_ref, lam_init).astype(o_ref.dtype)


def _diff_attn_sample(q, k_new, v_new, cache_k, cache_v, page_table, slopes, diff_lambda, subln_g,
                      layer, db, ds, lam_init, *, pages_per_step=4):
    dv = 2 * HEAD_DIM
    width = q.shape[1]
    n_heads = width // dv
    n_pages = page_table.shape[1]
    pg = math.gcd(pages_per_step, n_pages)
    q5 = q.reshape(db, ds, n_heads, 2, HEAD_DIM).transpose(0, 3, 2, 1, 4)
    kn5 = k_new.reshape(db, ds, n_heads, 2, HEAD_DIM).transpose(0, 3, 2, 1, 4)
    vn4 = v_new.reshape(db, ds, n_heads, dv).transpose(0, 2, 1, 3)
    map_spec = pl.BlockSpec((None, 2, n_heads, ds, HEAD_DIM), lambda b, s, pt: (b, 0, 0, 0, 0))
    head_spec = pl.BlockSpec((None, n_heads, ds, dv), lambda b, s, pt: (b, 0, 0, 0))
    ck = cache_k.transpose(0, 1, 3, 2, 4)
    cv = cache_v.transpose(0, 1, 3, 2, 4)

    def page_spec(g):
        return pl.BlockSpec((None, None, n_heads, PAGE_SIZE, dv),
                            lambda b, s, pt: (layer, pt[b, s * pg + g], 0, 0, 0))

    kern = functools.partial(_diff_sample_kernel, pages_per_step=pg, past=n_pages * PAGE_SIZE,
                             lam_init=lam_init)
    grid_spec = pltpu.PrefetchScalarGridSpec(
        num_scalar_prefetch=1,
        grid=(db, n_pages // pg),
        in_specs=[pl.BlockSpec((n_heads, 1, 1), lambda b, s, pt: (0, 0, 0)), map_spec, map_spec, head_spec]
        + [page_spec(g) for g in range(pg)] + [page_spec(g) for g in range(pg)]
        + [pl.BlockSpec((None, 4, HEAD_DIM), lambda b, s, pt: (layer, 0, 0)),
           pl.BlockSpec((None, 1, dv), lambda b, s, pt: (layer, 0, 0))],
        out_specs=head_spec,
        scratch_shapes=[pltpu.VMEM((2, n_heads, ds, 1), F32), pltpu.VMEM((2, n_heads, ds, 1), F32),
                        pltpu.VMEM((2, n_heads, ds, dv), F32)],
    )
    out = pl.pallas_call(
        kern,
        grid_spec=grid_spec,
        out_shape=jax.ShapeDtypeStruct((db, n_heads, ds, dv), BF16),
        compiler_params=_params(("arbitrary", "arbitrary")),
    )(page_table, slopes.reshape(n_heads, 1, 1), q5, kn5, vn4, *([ck] * pg), *([cv] * pg), diff_lambda,
      subln_g.reshape(subln_g.shape[0], 1, dv))
    return out.transpose(0, 2, 1, 3).reshape(db * ds, width)


def _top_blocks(gate, n_valid, ksel):
    nb = gate.shape[-1]
    lane = lax.broadcasted_iota(I32, gate.shape, gate.ndim - 1)
    lane_f = lane.astype(F32)
    valid = lane < n_valid
    g = jnp.where(valid, gate, NEG_INF)
    picks = []
    for _ in range(ksel):
        mx = jnp.max(g, axis=-1, keepdims=True)
        idx = jnp.min(jnp.where((g == mx) & valid, lane_f, float(nb)), axis=-1, keepdims=True)
        picks.append(idx)
        hit = lane_f == idx
        g = jnp.where(hit, NEG_INF, g)
        valid = valid & jnp.logical_not(hit)
    return picks


def _block_mean_kernel(k_ref, o_ref):
    o_ref[...] = jnp.mean(k_ref[...], axis=0, keepdims=True)


def _block_means(k32, rows):
    t, width = k32.shape
    out = pl.pallas_call(
        _block_mean_kernel,
        grid=(t // rows,),
        in_specs=[pl.BlockSpec((rows, width), lambda i: (i, 0))],
        out_specs=pl.BlockSpec((None, 1, width), lambda i: (i, 0, 0)),
        out_shape=jax.ShapeDtypeStruct((t // rows, 1, width), F32),
        compiler_params=_params(("arbitrary",)),
    )(k32)
    return out.reshape(t // rows, width)


def _heads(ref, rows, n, width):
    return jnp.stack([ref[rows, i * width:(i + 1) * width] for i in range(n)], axis=0)


def _moba_prompt_kernel(slopes_ref, q_ref, k_ref, v_ref, km_ref, o_ref, m_ref, l_ref, acc_ref, *, ksel,
                        group):
    qi = pl.program_id(2)
    blk = MOBA_BLOCK
    slope3 = slopes_ref[...]
    q3 = _heads(q_ref, slice(None), group, HEAD_DIM)
    row = lax.broadcasted_iota(I32, (blk, blk), 0)
    col = lax.broadcasted_iota(I32, (blk, blk), 1)
    rel = slope3 * (col - row).astype(F32)[None]

    d0 = pl.ds(pl.multiple_of(qi * blk, blk), blk)
    s = _bdot_t(q3, _heads(k_ref, d0, group, HEAD_DIM)) + rel
    s = jnp.where((col <= row)[None], s, NEG_INF)
    m0 = jnp.max(s, axis=-1, keepdims=True)
    p = jnp.exp(s - m0)
    m_ref[...] = m0
    l_ref[...] = jnp.sum(p, axis=-1, keepdims=True)
    acc_ref[...] = _bdot(p.astype(BF16), _heads(v_ref, d0, group, HEAD_DIM))

    gate = _bdot_t(q3, _heads(km_ref, slice(None), group, HEAD_DIM).astype(BF16))
    picks = _top_blocks(gate, qi, ksel)

    def body(kj, c):
        k0 = pl.ds(pl.multiple_of(kj * blk, blk), blk)
        kj_f = kj.astype(F32)
        chosen = picks[0] == kj_f
        for idx in picks[1:]:
            chosen = chosen | (idx == kj_f)
        row_bias = jnp.where(chosen, 0.0, NEG_INF) - slope3 * ((qi - kj) * blk).astype(F32)
        s = _bdot_t(q3, _heads(k_ref, k0, group, HEAD_DIM)) + rel + row_bias
        m_prev = m_ref[...]
        m_new = jnp.maximum(m_prev, jnp.max(s, axis=-1, keepdims=True))
        corr = jnp.exp(m_prev - m_new)
        p = jnp.exp(s - m_new)
        l_ref[...] = corr * l_ref[...] + jnp.sum(p, axis=-1, keepdims=True)
        acc_ref[...] = corr * acc_ref[...] + _bdot(p.astype(BF16), _heads(v_ref, k0, group, HEAD_DIM))
        m_ref[...] = m_new
        return c

    lax.fori_loop(0, qi, body, 0)
    out = acc_ref[...] / l_ref[...]
    for i in range(group):
        o_ref[:, i * HEAD_DIM:(i + 1) * HEAD_DIM] = out[i].astype(o_ref.dtype)


def _moba_prompt(q16, k16, v16, kmean, slopes, bsz, seq, *, group=4):
    width = q16.shape[1]
    n_heads = width // HEAD_DIM
    assert seq % MOBA_BLOCK == 0 and n_heads % group == 0
    nb = seq // MOBA_BLOCK
    gw = group * HEAD_DIM
    ksel = min(MOBA_TOPK, nb - 1)
    if ksel == 0:
        ksel = 1
    kern = functools.partial(_moba_prompt_kernel, ksel=ksel, group=group)
    return pl.pallas_call(
        kern,
        grid=(bsz, n_heads // group, nb),
        in_specs=[
            pl.BlockSpec((None, group, 1, 1), lambda b, h, i: (h, 0, 0, 0)),
            pl.BlockSpec((MOBA_BLOCK, gw), lambda b, h, i: (b * nb + i, h)),
            pl.BlockSpec((seq, gw), lambda b, h, i: (b, h)),
            pl.BlockSpec((seq, gw), lambda b, h, i: (b, h)),
            pl.BlockSpec((None, nb, gw), lambda b, h, i: (b, 0, h)),
        ],
        out_specs=pl.BlockSpec((MOBA_BLOCK, gw), lambda b, h, i: (b * nb + i, h)),
        out_shape=jax.ShapeDtypeStruct(q16.shape, BF16),
        scratch_shapes=[pltpu.VMEM((group, MOBA_BLOCK, 1), F32), pltpu.VMEM((group, MOBA_BLOCK, 1), F32),
                        pltpu.VMEM((group, MOBA_BLOCK, HEAD_DIM), F32)],
        compiler_params=_params(("arbitrary", "arbitrary", "arbitrary")),
    )(slopes.reshape(n_heads // group, group, 1, 1), q16, k16, v16, kmean)


def _page_block_mean_kernel(pt_ref, *refs, bp):
    o_ref = refs[bp]
    tot = None
    for g in range(bp):
        part = jnp.sum(refs[g][...], axis=0)
        tot = part if tot is None else tot + part
    o_ref[...] = tot * (1.0 / (bp * PAGE_SIZE))


def _paged_block_means(cache_k, page_table, layer, n_full):
    db = page_table.shape[0]
    bp = MOBA_BLOCK // PAGE_SIZE
    n_heads, dh = cache_k.shape[3], cache_k.shape[4]

    def page_spec(g):
        return pl.BlockSpec((None, None, PAGE_SIZE, n_heads, dh),
                            lambda b, n, pt: (layer, pt[b, n * bp + g], 0, 0, 0))

    grid_spec = pltpu.PrefetchScalarGridSpec(
        num_scalar_prefetch=1,
        grid=(db, n_full),
        in_specs=[page_spec(g) for g in range(bp)],
        out_specs=pl.BlockSpec((None, None, n_heads, dh), lambda b, n, pt: (b, n, 0, 0)),
    )
    return pl.pallas_call(
        functools.partial(_page_block_mean_kernel, bp=bp),
        grid_spec=grid_spec,
        out_shape=jax.ShapeDtypeStruct((db, n_full, n_heads, dh), F32),
        compiler_params=_params(("arbitrary", "arbitrary")),
    )(page_table, *([cache_k] * bp))


def _moba_select_kernel(q_ref, km_ref, o_ref, *, n_heads, ksel):
    n_full = km_ref.shape[1]
    lane = lax.broadcasted_iota(I32, (q_ref.shape[0], 128), 1)
    for h in range(n_heads):
        c0 = h * HEAD_DIM
        gate = _dot_t(q_ref[:, c0:c0 + HEAD_DIM], km_ref[h].astype(BF16))
        picks = _top_blocks(gate, n_full, ksel)
        out = jnp.zeros(lane.shape, I32)
        for j, idx in enumerate(picks):
            out = jnp.where(lane == j, idx.astype(I32), out)
        o_ref[h] = out


def _moba_select(q16, kmean, db, ds, ksel):
    width = q16.shape[1]
    n_heads = width // HEAD_DIM
    n_full = kmean.shape[2]
    out = pl.pallas_call(
        functools.partial(_moba_select_kernel, n_heads=n_heads, ksel=ksel),
        grid=(db,),
        in_specs=[pl.BlockSpec((ds, width), lambda b: (b, 0)),
                  pl.BlockSpec((None, n_heads, n_full, HEAD_DIM), lambda b: (b, 0, 0, 0))],
        out_specs=pl.BlockSpec((None, n_heads, ds, 128), lambda b: (b, 0, 0, 0)),
        out_shape=jax.ShapeDtypeStruct((db, n_heads, ds, 128), I32),
        compiler_params=_params(("arbitrary",)),
    )(q16, kmean)
    return out[..., :ksel]


def _moba_sample_kernel(pt_ref, sel_ref, slopes_ref, q_ref, kn_ref, vn_ref, ck_hbm, cv_hbm, o_ref,
                        kbuf, vbuf, ksem, vsem, *, layer, ksel, bp, past):
    b = pl.program_id(0)
    h = pl.program_id(1)
    n_heads = pl.num_programs(1)
    t = b * n_heads + h
    slot = lax.rem(t, 2)
    ds = q_ref.shape[0]
    n_tiles = ksel * bp

    def copies(bb, hh, sl):
        out = []
        base = (bb * n_heads + hh) * ds * ksel
        for qn in range(ds):
            for j in range(ksel):
                blk = sel_ref[base + qn * ksel + j]
                for g in range(bp):
                    page = pt_ref[bb, blk * bp + g]
                    out.append(pltpu.make_async_copy(ck_hbm.at[layer, page, :, hh, :],
                                                     kbuf.at[sl, qn, j * bp + g], ksem.at[sl]))
                    out.append(pltpu.make_async_copy(cv_hbm.at[layer, page, :, hh, :],
                                                     vbuf.at[sl, qn, j * bp + g], vsem.at[sl]))
        return out

    @pl.when(t == 0)
    def _():
        for c in copies(b, h, slot):
            c.start()

    @pl.when(t + 1 < pl.num_programs(0) * n_heads)
    def _():
        wrap = h + 1 == n_heads
        for c in copies(jnp.where(wrap, b + 1, b), jnp.where(wrap, 0, h + 1), 1 - slot):
            c.start()

    for c in copies(b, h, slot):
        c.wait()

    slope = slopes_ref[h]
    q16 = q_ref[...]
    k3 = kbuf[slot].reshape(ds, n_tiles * PAGE_SIZE, HEAD_DIM).astype(BF16)
    v3 = vbuf[slot].reshape(ds, n_tiles * PAGE_SIZE, HEAD_DIM).astype(BF16)
    q3 = jnp.broadcast_to(q16[None], (ds, ds, HEAD_DIM))
    row = lax.broadcasted_iota(I32, (ds, PAGE_SIZE), 0)
    col = lax.broadcasted_iota(I32, (ds, PAGE_SIZE), 1)
    base = t * (ds * ksel)
    dist_rows = []
    for qn in range(ds):
        pieces = []
        for j in range(ksel):
            blk_start = sel_ref[base + qn * ksel + j] * MOBA_BLOCK
            for g in range(bp):
                pieces.append((row - col + (past - blk_start - g * PAGE_SIZE)).astype(F32))
        dist_rows.append(jnp.concatenate(pieces, axis=-1))
    dist3 = jnp.stack(dist_rows, axis=0)
    s = _bdot_t(q3, k3) - slope * dist3
    r8 = lax.broadcasted_iota(I32, (ds, ds), 0)
    c8 = lax.broadcasted_iota(I32, (ds, ds), 1)
    s_own = _dot_t(q16, kn_ref[...]) - slope * (r8 - c8).astype(F32)
    s_own = jnp.where(c8 <= r8, s_own, NEG_INF)
    s_own3 = jnp.broadcast_to(s_own[None], (ds, ds, ds))
    mx = jnp.maximum(jnp.max(s, axis=-1, keepdims=True), jnp.max(s_own3, axis=-1, keepdims=True))
    p = jnp.exp(s - mx)
    p_own = jnp.exp(s_own3 - mx)
    den = jnp.sum(p, axis=-1, keepdims=True) + jnp.sum(p_own, axis=-1, keepdims=True)
    out = _bdot((p / den).astype(BF16), v3)
    own = jnp.dot((p_own / den).reshape(ds * ds, ds).astype(BF16), vn_ref[...], preferred_element_type=F32)
    out = out + own.reshape(ds, ds, HEAD_DIM)
    keep = lax.broadcasted_iota(I32, (ds, HEAD_DIM), 0)
    res = jnp.zeros((ds, HEAD_DIM), F32)
    for qn in range(ds):
        res = res + jnp.where(keep == qn, out[qn], 0.0)
    o_ref[...] = res.astype(o_ref.dtype)


def _moba_sample(q16, k_new, v_new, cache_k, cache_v, page_table, sel, slopes, layer, db, ds):
    width = q16.shape[1]
    n_heads = width // HEAD_DIM
    ksel = sel.shape[-1]
    bp = MOBA_BLOCK // PAGE_SIZE
    n_pages = page_table.shape[1]
    n_tiles = ksel * bp
    head_spec = pl.BlockSpec((ds, HEAD_DIM), lambda b, h, pt, sl: (b, h))
    hbm = pl.BlockSpec(memory_space=pl.ANY)
    tile_buf = pltpu.VMEM((2, ds, n_tiles, PAGE_SIZE, HEAD_DIM), F32)
    grid_spec = pltpu.PrefetchScalarGridSpec(
        num_scalar_prefetch=2,
        grid=(db, n_heads),
        in_specs=[pl.BlockSpec(memory_space=pltpu.SMEM), head_spec, head_spec, head_spec, hbm, hbm],
        out_specs=head_spec,
        scratch_shapes=[tile_buf, tile_buf, pltpu.SemaphoreType.DMA((2,)), pltpu.SemaphoreType.DMA((2,))],
    )
    kern = functools.partial(_moba_sample_kernel, layer=layer, ksel=ksel, bp=bp, past=n_pages * PAGE_SIZE)
    return pl.pallas_call(
        kern,
        grid_spec=grid_spec,
        out_shape=jax.ShapeDtypeStruct(q16.shape, BF16),
        compiler_params=_params(("arbitrary", "arbitrary")),
    )(page_table, sel.reshape(-1), slopes, q16, k_new, v_new, cache_k, cache_v)


def _layer_norm_rows(z, g_ref, b_ref):
    mu = jnp.mean(z, axis=-1, keepdims=True)
    zc = z - mu
    var = jnp.mean(zc * zc, axis=-1, keepdims=True)
    return zc * lax.rsqrt(var + LN_EPS) * g_ref[...] + b_ref[...]


def _route(logits):
    lane = lax.broadcasted_iota(I32, logits.shape, 1)
    lane_f = lane.astype(F32)
    none = float(ROUTER_LANES)
    is_g = lane < MOE_GROUPS
    lg = jnp.where(is_g, logits, NEG_INF)
    mg = jnp.max(lg, axis=-1, keepdims=True)
    g_top = jnp.min(jnp.where(lg == mg, lane_f, none), axis=-1, keepdims=True)
    gate_g = 1.0 / jnp.sum(jnp.where(is_g, jnp.exp(logits - mg), 0.0), axis=-1, keepdims=True)
    e_lo = MOE_GROUPS + g_top * MOE_PER_GROUP
    le = jnp.where((lane_f >= e_lo) & (lane_f < e_lo + MOE_PER_GROUP), logits, NEG_INF)
    v1 = jnp.max(le, axis=-1, keepdims=True)
    i1 = jnp.min(jnp.where(le == v1, lane_f, none), axis=-1, keepdims=True)
    le2 = jnp.where(lane_f == i1, NEG_INF, le)
    v2 = jnp.max(le2, axis=-1, keepdims=True)
    i2 = jnp.min(jnp.where(le2 == v2, lane_f, none), axis=-1, keepdims=True)
    e2 = jnp.exp(v2 - v1)
    w1 = gate_g / (1.0 + e2)
    w2 = gate_g * e2 / (1.0 + e2)
    eid = jnp.where(lane == 0, i1 - MOE_GROUPS, jnp.where(lane == 1, i2 - MOE_GROUPS, 0.0)).astype(I32)
    wgt = jnp.where(lane == 0, w1, jnp.where(lane == 1, w2, 0.0))
    return eid, wgt


def _ln_router_kernel(z_ref, g_ref, b_ref, rw_ref, rb_ref, y32_ref, eid_ref, wgt_ref):
    y = _layer_norm_rows(z_ref[...], g_ref, b_ref)
    y32_ref[...] = y
    logits = jnp.dot(y.astype(BF16), rw_ref[...], preferred_element_type=F32) + rb_ref[...]
    eid, wgt = _route(logits)
    eid_ref[...] = eid
    wgt_ref[...] = wgt


def _ln_router(z, ln_g, ln_b, layer, which, rw, rb, *, tm=256):
    t, d = z.shape
    tm = min(tm, t)
    assert t % tm == 0
    row = pl.BlockSpec((tm, d), lambda i: (i, 0))
    vec = pl.BlockSpec((None, None, 1, d), lambda i: (layer, which, 0, 0))
    lane_blk = pl.BlockSpec((tm, ROUTER_LANES), lambda i: (i, 0))
    return pl.pallas_call(
        _ln_router_kernel,
        grid=(t // tm,),
        in_specs=[row, vec, vec, pl.BlockSpec((d, ROUTER_LANES), lambda i: (0, 0)),
                  pl.BlockSpec((1, ROUTER_LANES), lambda i: (0, 0))],
        out_specs=[row, lane_blk, lane_blk],
        out_shape=[jax.ShapeDtypeStruct((t, d), F32),
                   jax.ShapeDtypeStruct((t, ROUTER_LANES), I32),
                   jax.ShapeDtypeStruct((t, ROUTER_LANES), F32)],
        compiler_params=_params(("arbitrary",)),
    )(z, ln_g.reshape(ln_g.shape[0], ln_g.shape[1], 1, d), ln_b.reshape(ln_b.shape[0], ln_b.shape[1], 1, d),
      rw, rb)


def _ln_combine_kernel(x_ref, y0_ref, y1_ref, w_ref, g_ref, b_ref, y32_ref, y16_ref, *, alpha):
    w = w_ref[...]
    z = alpha * x_ref[...] + (w[:, 0:1] * y0_ref[...] + w[:, 1:2] * y1_ref[...])
    y = _layer_norm_rows(z, g_ref, b_ref)
    y32_ref[...] = y
    y16_ref[...] = y.astype(BF16)


def _ln_combine(x, y, wgt, ln_g, ln_b, layer, which, alpha, *, tm=256):
    t, d = x.shape
    tm = min(tm, t)
    assert t % tm == 0
    row = pl.BlockSpec((tm, d), lambda i: (i, 0))
    vec = pl.BlockSpec((None, None, 1, d), lambda i: (layer, which, 0, 0))
    return pl.pallas_call(
        functools.partial(_ln_combine_kernel, alpha=alpha),
        grid=(t // tm,),
        in_specs=[row, pl.BlockSpec((None, tm, d), lambda i: (0, i, 0)),
                  pl.BlockSpec((None, tm, d), lambda i: (1, i, 0)),
                  pl.BlockSpec((tm, ROUTER_LANES), lambda i: (i, 0)), vec, vec],
        out_specs=[row, row],
        out_shape=[jax.ShapeDtypeStruct((t, d), F32), jax.ShapeDtypeStruct((t, d), BF16)],
        compiler_params=_params(("arbitrary",)),
    )(x, y, y, wgt, ln_g.reshape(ln_g.shape[0], ln_g.shape[1], 1, d),
      ln_b.reshape(ln_b.shape[0], ln_b.shape[1], 1, d))


def _moe_kernel(te_ref, nv_ref, tab_ref, x_hbm, w1_ref, w3_ref, w2_ref, out_hbm, xbuf, obuf, gsem, ssem, *,
                n_asg):
    t = pl.program_id(0)
    c = pl.program_id(1)
    n_tiles = pl.num_programs(0)
    n_chunks = pl.num_programs(1)
    nv = nv_ref[0]
    sl = lax.rem(t, 2)
    n_tok = n_asg // MOE_TOPK

    def slot_asg(tile, r):
        idx = tile * MOE_TILE + r
        word = tab_ref[lax.shift_right_logical(idx, 1)]
        return jnp.where((idx & 1) == 0, word & 0xFFFF, lax.shift_right_logical(word, 16))

    def gather(tile, buf, start):
        def body(r, carry):
            tok = lax.shift_right_logical(jnp.minimum(slot_asg(tile, r), n_asg - 1), 1)
            cp = pltpu.make_async_copy(x_hbm.at[pl.ds(tok, 1)], xbuf.at[buf, pl.ds(r, 1)], gsem.at[buf])
            if start:
                cp.start()
            else:
                cp.wait()
            return carry
        lax.fori_loop(0, MOE_TILE, body, 0, unroll=8)

    def scatter(tile, buf, start):
        def body(r, carry):
            a = slot_asg(tile, r)

            @pl.when(a < n_asg)
            def _():
                row = (a & 1) * n_tok + lax.shift_right_logical(a, 1)
                cp = pltpu.make_async_copy(obuf.at[buf, pl.ds(r, 1)], out_hbm.at[pl.ds(row, 1)], ssem.at[buf])
                if start:
                    cp.start()
                else:
                    cp.wait()
            return carry
        lax.fori_loop(0, MOE_TILE, body, 0, unroll=8)

    @pl.when(c == 0)
    def _():
        @pl.when(t == 0)
        def _():
            gather(t, sl, True)

        @pl.when(t < nv)
        def _():
            gather(t, sl, False)

        @pl.when(t + 1 < nv)
        def _():
            gather(t + 1, 1 - sl, True)

        @pl.when((t >= 2) & (t - 2 < nv))
        def _():
            scatter(t - 2, sl, False)

    @pl.when(t < nv)
    def _():
        x = xbuf[sl].astype(BF16)
        a = jnp.dot(x, w1_ref[...].astype(BF16), preferred_element_type=F32)
        b = jnp.dot(x, w3_ref[...].astype(BF16), preferred_element_type=F32)
        hid = (a * jax.nn.sigmoid(a)) * b
        y = jnp.dot(hid.astype(BF16), w2_ref[...].astype(BF16), preferred_element_type=F32)

        @pl.when(c == 0)
        def _():
            obuf[sl] = y

        @pl.when(c != 0)
        def _():
            obuf[sl] += y

        @pl.when(c == n_chunks - 1)
        def _():
            scatter(t, sl, True)

    @pl.when((t == n_tiles - 1) & (c == n_chunks - 1))
    def _():
        @pl.when((t >= 1) & (t - 1 < nv))
        def _():
            scatter(t - 1, 1 - sl, False)

        @pl.when(t < nv)
        def _():
            scatter(t, sl, False)


def _moe_experts(x32, slot_table, tile_expert, n_valid, w1, w3, w2, layer, n_tiles, *, n_chunks=2):
    n_tok, d = x32.shape
    n_asg = n_tok * MOE_TOPK
    d_exp = w1.shape[-1]
    ce = d_exp // n_chunks

    def chunk(t, c, nv):
        return jnp.where(t >= nv[0], 0, jnp.where(t % 2 == 0, c, n_chunks - 1 - c))

    grid_spec = pltpu.PrefetchScalarGridSpec(
        num_scalar_prefetch=3,
        grid=(n_tiles, n_chunks),
        in_specs=[
            pl.BlockSpec(memory_space=pl.ANY),
            pl.BlockSpec((None, None, d, ce), lambda t, c, te, nv, tab: (layer, te[t], 0, chunk(t, c, nv))),
            pl.BlockSpec((None, None, d, ce), lambda t, c, te, nv, tab: (layer, te[t], 0, chunk(t, c, nv))),
            pl.BlockSpec((None, None, ce, d), lambda t, c, te, nv, tab: (layer, te[t], chunk(t, c, nv), 0)),
        ],
        out_specs=pl.BlockSpec(memory_space=pl.ANY),
        scratch_shapes=[pltpu.VMEM((2, MOE_TILE, d), F32), pltpu.VMEM((2, MOE_TILE, d), F32),
                        pltpu.SemaphoreType.DMA((2,)), pltpu.SemaphoreType.DMA((2,))],
    )
    return pl.pallas_call(
        functools.partial(_moe_kernel, n_asg=n_asg),
        grid_spec=grid_spec,
        out_shape=jax.ShapeDtypeStruct((n_asg, d), F32),
        compiler_params=_params(("arbitrary", "arbitrary")),
    )(tile_expert, n_valid, slot_table, x32, w1, w3, w2)


def _moe_layer(x32, eid, wgt, ln_g, ln_b, layer, w1, w3, w2, alpha):
    n_tok, d = x32.shape
    n_asg = n_tok * MOE_TOPK
    assert n_asg < (1 << 16) and MOE_TOPK == 2
    e_flat = eid[:, :MOE_TOPK].reshape(n_asg)
    onehot = (e_flat[:, None] == jnp.arange(N_EXPERTS, dtype=I32)[None, :]).astype(I32)
    csum = jnp.cumsum(onehot, axis=0)
    rank = jnp.sum(csum * onehot, axis=1) - 1
    counts = csum[-1]
    padded = (counts + MOE_TILE - 1) // MOE_TILE * MOE_TILE
    pad_ends = jnp.cumsum(padded)
    dest = (pad_ends - padded)[e_flat] + rank
    n_tiles = -(-(n_asg + N_EXPERTS * (MOE_TILE - 1)) // MOE_TILE)
    slot_asg = jnp.full((n_tiles * MOE_TILE,), n_asg, I32).at[dest].set(jnp.arange(n_asg, dtype=I32))
    slot_table = slot_asg[0::2] | (slot_asg[1::2] << 16)
    tile_start = jnp.arange(n_tiles, dtype=I32) * MOE_TILE
    n_valid = (pad_ends[-1] // MOE_TILE).astype(I32)
    tile_expert = jnp.searchsorted(pad_ends, jnp.minimum(tile_start, pad_ends[-1] - 1), side="right")
    tile_expert = jnp.minimum(tile_expert, N_EXPERTS - 1).astype(I32)
    y = _moe_experts(x32, slot_table, tile_expert, n_valid.reshape(1), w1, w3, w2, layer, n_tiles)
    y = y.reshape(MOE_TOPK, n_tok, d)
    return _ln_combine(x32, y, wgt, ln_g, ln_b, layer, 1, alpha, tm=_row_tile(n_tok, 256))


def _router_weights(router_g_w, router_g_b, router_e_w, router_e_b, layer):
    d = router_g_w.shape[1]
    pad = ROUTER_LANES - MOE_GROUPS - N_EXPERTS
    rw = jnp.concatenate([router_g_w[layer], router_e_w[layer], jnp.zeros((d, pad), F32)], axis=1).astype(BF16)
    rb = jnp.concatenate([router_g_b[layer], router_e_b[layer], jnp.zeros((pad,), F32)])[None, :]
    return rw, rb


def kernel(x_prompt, x_sample, cache_k_diff, cache_v_diff, state_pool, cache_k_moba, cache_v_moba, page_table, w_in_even, pool_w, pool_scale, diff_lambda, diff_subln_g, w_out_even, w_in_odd, w_out_odd, ln_g, ln_b, router_g_w, router_g_b, router_e_w, router_e_b, moe_w1, moe_w3, moe_w2):
    bsz, seq, d = x_prompt.shape
    db, ds, _ = x_sample.shape
    depth = ln_g.shape[0]
    alpha = (2.0 * depth) ** 0.25
    n_pages = page_table.shape[1]
    past = n_pages * PAGE_SIZE
    tp, ts = bsz * seq, db * ds
    pool_width = pool_scale.shape[1]
    diff_width = cache_k_diff.shape[3] * cache_k_diff.shape[4]
    moba_width = cache_k_moba.shape[3] * cache_k_moba.shape[4]
    slopes_diff = jnp.asarray(_alibi_slopes(cache_k_diff.shape[3]))
    slopes_moba = jnp.asarray(_alibi_slopes(cache_k_moba.shape[3]))
    scale = HEAD_DIM ** -0.5
    tm_p = 512

    xp32, xs32 = x_prompt.reshape(tp, d), x_sample.reshape(ts, d)
    assert tp % ts == 0
    groups = ((xp32.astype(BF16), xp32, 0, tp, tm_p, True), (xs32.astype(BF16), xs32, 0, ts, ts, False))
    kd_p, vd_p, pl_p, km_p, vm_p = [], [], [], [], []
    kd_s, vd_s, pl_s, km_s, vm_s = [], [], [], [], []
    for i in range(depth):
        j = i // 2
        if i % 2 == 0:
            lam_init = 0.8 - 0.6 * math.exp(-0.3 * i)
            zs = []
            for x16, x32, row0, m, tm, is_prompt in groups:
                (u32,) = _matmul([x16], w_in_even, j, 0, pool_width, [(F32, 1.0)], tm=tm, m=m,
                                 x_row0=row0)
                (q16,) = _matmul([x16], w_in_even, j, pool_width, diff_width, [(BF16, scale)], tm=tm, m=m,
                                 x_row0=row0)
                k32, k16 = _matmul([x16], w_in_even, j, pool_width + diff_width, diff_width,
                                   [(F32, 1.0), (BF16, 1.0)], tm=tm, m=m, x_row0=row0)
                v32, v16 = _matmul([x16], w_in_even, j, pool_width + 2 * diff_width, diff_width,
                                   [(F32, 1.0), (BF16, 1.0)], tm=tm, m=m, x_row0=row0)
                if is_prompt:
                    u3 = u32.reshape(bsz, seq, pool_width)
                    prefix = jnp.zeros((bsz, POOL_HALO, pool_width), F32)
                    pool = _pool_mix(u3, prefix, pool_w, pool_scale, j, 0, tl=256)
                    att = _diff_attn_prompt(q16, k16, v16, slopes_diff, diff_lambda, diff_subln_g, j,
                                            bsz, seq, lam_init)
                    ext = u3 if seq >= POOL_CARRY else jnp.concatenate([prefix[:, :POOL_CARRY], u3], axis=1)
                    kd_p.append(k32.reshape(bsz, seq, -1, 2 * HEAD_DIM))
                    vd_p.append(v32.reshape(bsz, seq, -1, 2 * HEAD_DIM))
                    pl_p.append(ext[:, -POOL_CARRY:])
                else:
                    u3 = u32.reshape(db, ds, pool_width)
                    prefix = jnp.concatenate([jnp.zeros((db, POOL_HALO - POOL_CARRY, pool_width), F32),
                                              state_pool[j]], axis=1)
                    pool = _pool_mix(u3, prefix, pool_w, pool_scale, j, past, tl=ds)
                    att = _diff_attn_sample(q16, k16, v16, cache_k_diff, cache_v_diff, page_table,
                                            slopes_diff, diff_lambda, diff_subln_g, j, db, ds, lam_init)
                    ext = jnp.concatenate([state_pool[j], u3], axis=1)
                    kd_s.append(k32.reshape(db, ds, -1, 2 * HEAD_DIM))
                    vd_s.append(v32.reshape(db, ds, -1, 2 * HEAD_DIM))
                    pl_s.append(ext[:, -POOL_CARRY:])
                (z,) = _matmul([pool, att], w_out_even, j, 0, d, [(F32, 1.0)], tm=tm, resid=x32, alpha=alpha,
                               resid_row0=row0)
                zs.append(z)
        else:
            zs = []
            for x16, x32, row0, m, tm, is_prompt in groups:
                (q16,) = _matmul([x16], w_in_odd, j, 0, moba_width, [(BF16, scale)], tm=tm, m=m, x_row0=row0)
                k32, k16 = _matmul([x16], w_in_odd, j, moba_width, moba_width, [(F32, 1.0), (BF16, 1.0)], tm=tm,
                                   m=m, x_row0=row0)
                v32, v16 = _matmul([x16], w_in_odd, j, 2 * moba_width, moba_width, [(F32, 1.0), (BF16, 1.0)],
                                   tm=tm, m=m, x_row0=row0)
                if is_prompt:
                    kmean = _block_means(k32, MOBA_BLOCK).reshape(bsz, seq // MOBA_BLOCK, moba_width)
                    att = _moba_prompt(q16, k16, v16, kmean, slopes_moba, bsz, seq)
                    km_p.append(k32.reshape(bsz, seq, -1, HEAD_DIM))
                    vm_p.append(v32.reshape(bsz, seq, -1, HEAD_DIM))
                else:
                    n_full = past // MOBA_BLOCK
                    assert past == n_full * MOBA_BLOCK and n_full >= 1
                    ksel = min(MOBA_TOPK, n_full)
                    kmean = _paged_block_means(cache_k_moba, page_table, j, n_full)
                    sel = _moba_select(q16, kmean.transpose(0, 2, 1, 3), db, ds, ksel)
                    att = _moba_sample(q16, k16, v16, cache_k_moba, cache_v_moba, page_table,
                                       sel, slopes_moba, j, db, ds)
                    km_s.append(k32.reshape(db, ds, -1, HEAD_DIM))
                    vm_s.append(v32.reshape(db, ds, -1, HEAD_DIM))
                (z,) = _matmul([att], w_out_odd, j, 0, d, [(F32, 1.0)], tm=tm, resid=x32, alpha=alpha,
                               resid_row0=row0)
                zs.append(z)
        rw, rb = _router_weights(router_g_w, router_g_b, router_e_w, router_e_b, i)
        outs = [_ln_router(z, ln_g, ln_b, i, 0, rw, rb) for z in zs]
        x32 = jnp.concatenate([o[0] for o in outs], axis=0)
        eid = jnp.concatenate([o[1] for o in outs], axis=0)
        wgt = jnp.concatenate([o[2] for o in outs], axis=0)
        y32, y16 = _moe_layer(x32, eid, wgt, ln_g, ln_b, i, moe_w1, moe_w3, moe_w2, alpha)
        groups = ((y16, y32, 0, tp, tm_p, True), (y16, y32, tp, ts, ts, False))
    return (y32[:tp].reshape(bsz, seq, d), y32[tp:].reshape(db, ds, d),
            jnp.stack(kd_p), jnp.stack(vd_p), jnp.stack(pl_p), jnp.stack(km_p), jnp.stack(vm_p),
            jnp.stack(kd_s), jnp.stack(vd_s), jnp.stack(pl_s), jnp.stack(km_s), jnp.stack(vm_s))
```

```python
import functools
import math

import numpy as np
import jax
import jax.numpy as jnp
from jax import lax
from jax.experimental import pallas as pl
from jax.experimental.pallas import tpu as pltpu

F32 = jnp.float32
BF16 = jnp.bfloat16
I32 = jnp.int32

HEAD_DIM = 128
POOL_WINDOWS = (2, 4, 8, 16)
POOL_CARRY = max(POOL_WINDOWS) - 1
POOL_HALO = 16
MOBA_BLOCK = 256
MOBA_TOPK = 3
PAGE_SIZE = 128
MOE_GROUPS = 4
MOE_PER_GROUP = 8
N_EXPERTS = MOE_GROUPS * MOE_PER_GROUP
MOE_TOPK = 2
LN_EPS = 1e-5
RMS_EPS = 1e-5
NEG_INF = float("-inf")

MOE_TILE = 256
ROUTER_LANES = 128
VMEM_LIMIT = 56 * 1024 * 1024


def _alibi_slopes(n):
    def pow2(m):
        start = 2.0 ** (-(2.0 ** -(math.log2(m) - 3)))
        return [start ** (i + 1) for i in range(m)]
    if (n & (n - 1)) == 0:
        s = pow2(n)
    else:
        c = 2 ** int(math.floor(math.log2(n)))
        s = pow2(c) + pow2(2 * c)[0::2][: n - c]
    return np.asarray(s, np.float32)


def _row_tile(n, cap):
    best = max(t for t in range(16, cap + 1, 16) if n % t == 0)
    return best


def _params(semantics):
    return pltpu.CompilerParams(dimension_semantics=semantics, vmem_limit_bytes=VMEM_LIMIT)


def _dot_t(a, b, precision=None):
    return lax.dot_general(a, b, (((1,), (1,)), ((), ())), preferred_element_type=F32,
                           precision=precision)


def _bdot_t(a, b):
    return lax.dot_general(a, b, (((2,), (2,)), ((0,), (0,))), preferred_element_type=F32)


def _bdot(a, b):
    return lax.dot_general(a, b, (((2,), (1,)), ((0,), (0,))), preferred_element_type=F32)


def _mm_kernel(*refs, k_splits, n_out, has_resid, alpha, scales, cast_rows):
    n_x = len(k_splits)
    x_refs = refs[:n_x]
    w_ref = refs[n_x]
    pos = n_x + 1
    resid_ref = refs[pos] if has_resid else None
    pos += int(has_resid)
    out_refs = refs[pos:pos + n_out]
    wbf_ref = refs[pos + n_out]

    @pl.when(pl.program_id(1) == 0)
    def _():
        def body(r, c):
            rows = pl.ds(pl.multiple_of(r * cast_rows, cast_rows), cast_rows)
            wbf_ref[rows, :] = w_ref[rows, :].astype(BF16)
            return c
        lax.fori_loop(0, w_ref.shape[0] // cast_rows, body, 0)

    acc = None
    for x_ref, (k0, k1) in zip(x_refs, k_splits):
        part = jnp.dot(x_ref[...], wbf_ref[k0:k1, :], preferred_element_type=F32)
        acc = part if acc is None else acc + part
    if has_resid:
        acc = alpha * resid_ref[...] + acc
    for o_ref, sc in zip(out_refs, scales):
        o_ref[...] = (acc if sc == 1.0 else acc * sc).astype(o_ref.dtype)


def _matmul(xs, w, layer, col0, ncols, outs, *, tm, tn=512, resid=None, alpha=1.0, m=None, x_row0=0,
            resid_row0=0):
    m = xs[0].shape[0] if m is None else m
    k_total = w.shape[1]
    tm = min(tm, m)
    assert m % tm == 0 and ncols % tn == 0 and col0 % tn == 0
    assert x_row0 % tm == 0 and resid_row0 % tm == 0
    xb0, rb0 = x_row0 // tm, resid_row0 // tm
    k_splits, k0 = [], 0
    for x in xs:
        k_splits.append((k0, k0 + x.shape[1]))
        k0 += x.shape[1]
    assert k0 == k_total
    cb = col0 // tn
    in_specs = [pl.BlockSpec((tm, x.shape[1]), lambda j, i: (xb0 + i, 0)) for x in xs]
    in_specs.append(pl.BlockSpec((None, k_total, tn), lambda j, i: (layer, 0, cb + j)))
    args = list(xs) + [w]
    if resid is not None:
        in_specs.append(pl.BlockSpec((tm, tn), lambda j, i: (rb0 + i, j)))
        args.append(resid)
    kern = functools.partial(
        _mm_kernel, k_splits=tuple(k_splits), n_out=len(outs), has_resid=resid is not None,
        alpha=alpha, scales=tuple(s for _, s in outs), cast_rows=min(512, k_total))
    res = pl.pallas_call(
        kern,
        grid=(ncols // tn, m // tm),
        in_specs=in_specs,
        out_specs=[pl.BlockSpec((tm, tn), lambda j, i: (i, j)) for _ in outs],
        out_shape=[jax.ShapeDtypeStruct((m, ncols), dt) for dt, _ in outs],
        scratch_shapes=[pltpu.VMEM((k_total, tn), BF16)],
        compiler_params=_params(("arbitrary", "arbitrary")),
    )(*args)
    return res


def _pool_kernel(u_ref, pre_ref, w_ref, sc_ref, o_ref, ext_ref, *, tl, pos0, group):
    li = pl.program_id(1)

    @pl.when(li == 0)
    def _():
        ext_ref[0:POOL_HALO, :] = pre_ref[...]

    ext_ref[POOL_HALO:POOL_HALO + tl, :] = u_ref[...]
    pos = pos0 + li * tl + lax.broadcasted_iota(I32, (tl, 1), 0)
    for g, win in enumerate(POOL_WINDOWS):
        c0, c1 = g * group, (g + 1) * group
        cur = ext_ref[POOL_HALO:POOL_HALO + tl, c0:c1]
        tot = cur
        for i in range(1, win):
            tot = tot + ext_ref[POOL_HALO - i:POOL_HALO - i + tl, c0:c1]
        cnt = jnp.minimum(win, pos + 1).astype(F32)
        pooled = tot / cnt - cur
        y = jnp.dot(pooled.astype(BF16), w_ref[g].astype(BF16), preferred_element_type=F32)
        o_ref[:, c0:c1] = (y * sc_ref[:, c0:c1]).astype(o_ref.dtype)
    if tl >= POOL_HALO:
        ext_ref[0:POOL_HALO, :] = ext_ref[tl:tl + POOL_HALO, :]


def _pool_mix(u, prefix, pool_w, pool_scale, layer, pos0, *, tl):
    bsz, length, width = u.shape
    tl = min(tl, length)
    assert length % tl == 0 and (length == tl or tl >= POOL_HALO)
    group = width // len(POOL_WINDOWS)
    kern = functools.partial(_pool_kernel, tl=tl, pos0=pos0, group=group)
    out = pl.pallas_call(
        kern,
        grid=(bsz, length // tl),
        in_specs=[
            pl.BlockSpec((None, tl, width), lambda b, l: (b, l, 0)),
            pl.BlockSpec((None, POOL_HALO, width), lambda b, l: (b, 0, 0)),
            pl.BlockSpec((None, len(POOL_WINDOWS), group, group), lambda b, l: (layer, 0, 0, 0)),
            pl.BlockSpec((None, 1, width), lambda b, l: (layer, 0, 0)),
        ],
        out_specs=pl.BlockSpec((None, tl, width), lambda b, l: (b, l, 0)),
        out_shape=jax.ShapeDtypeStruct((bsz, length, width), BF16),
        scratch_shapes=[pltpu.VMEM((POOL_HALO + tl, width), F32)],
        compiler_params=_params(("arbitrary", "arbitrary")),
    )(u, prefix, pool_w, pool_scale.reshape(pool_scale.shape[0], 1, width))
    return out.reshape(bsz * length, width)


def _diff_combine(o1, o2, lam_ref, g_ref, lam_init):
    lam = lam_ref[...]
    e1 = jnp.exp(jnp.sum(lam[0:1] * lam[1:2], axis=-1, keepdims=True))
    e2 = jnp.exp(jnp.sum(lam[2:3] * lam[3:4], axis=-1, keepdims=True))
    lam_full = e1 - e2 + lam_init
    d = o1 - lam_full * o2
    ms = jnp.mean(d * d, axis=-1, keepdims=True)
    return d * lax.rsqrt(ms + RMS_EPS) * g_ref[...] * (1.0 - lam_init)


def _diff_prompt_kernel(slopes_ref, q_ref, k_ref, v_ref, lam_ref, g_ref, o_ref, m_ref, l_ref, acc_ref, *,
                        tq, lam_init, group):
    qi = pl.program_id(2)
    dv = 2 * HEAD_DIM
    nmap = 2 * group
    slope3 = slopes_ref[...]
    row = lax.broadcasted_iota(I32, (tq, tq), 0)
    col = lax.broadcasted_iota(I32, (tq, tq), 1)
    rel = slope3 * (col - row).astype(F32)[None]
    q3 = _heads(q_ref, slice(None), nmap, HEAD_DIM)

    def values(rows):
        return jnp.stack([v_ref[rows, (j // 2) * dv:(j // 2 + 1) * dv] for j in range(nmap)], axis=0)

    d0 = pl.ds(pl.multiple_of(qi * tq, tq), tq)
    s = _bdot_t(q3, _heads(k_ref, d0, nmap, HEAD_DIM)) + rel
    s = jnp.where((col <= row)[None], s, NEG_INF)
    m0 = jnp.max(s, axis=-1, keepdims=True)
    p = jnp.exp(s - m0)
    m_ref[...] = m0
    l_ref[...] = jnp.sum(p, axis=-1, keepdims=True)
    acc_ref[...] = _bdot(p.astype(BF16), values(d0))

    def body(kj, c):
        k0 = pl.ds(pl.multiple_of(kj * tq, tq), tq)
        s = _bdot_t(q3, _heads(k_ref, k0, nmap, HEAD_DIM)) + (rel - slope3 * ((qi - kj) * tq).astype(F32))
        m_prev = m_ref[...]
        m_new = jnp.maximum(m_prev, jnp.max(s, axis=-1, keepdims=True))
        corr = jnp.exp(m_prev - m_new)
        p = jnp.exp(s - m_new)
        l_ref[...] = corr * l_ref[...] + jnp.sum(p, axis=-1, keepdims=True)
        acc_ref[...] = corr * acc_ref[...] + _bdot(p.astype(BF16), values(k0))
        m_ref[...] = m_new
        return c

    lax.fori_loop(0, qi, body, 0)
    o = acc_ref[...] / l_ref[...]
    for i in range(group):
        o_ref[:, i * dv:(i + 1) * dv] = _diff_combine(o[2 * i], o[2 * i + 1], lam_ref, g_ref,
                                                       lam_init).astype(o_ref.dtype)


def _diff_attn_prompt(q, k, v, slopes, diff_lambda, subln_g, layer, bsz, seq, lam_init, *, tq=256, group=4):
    dv = 2 * HEAD_DIM
    n_heads = q.shape[1] // dv
    tq = min(tq, seq)
    assert seq % tq == 0 and n_heads % group == 0
    nq = seq // tq
    gw = group * dv
    kern = functools.partial(_diff_prompt_kernel, tq=tq, lam_init=lam_init, group=group)
    return pl.pallas_call(
        kern,
        grid=(bsz, n_heads // group, nq),
        in_specs=[
            pl.BlockSpec((None, 2 * group, 1, 1), lambda b, h, i: (h, 0, 0, 0)),
            pl.BlockSpec((tq, gw), lambda b, h, i: (b * nq + i, h)),
            pl.BlockSpec((seq, gw), lambda b, h, i: (b, h)),
            pl.BlockSpec((seq, gw), lambda b, h, i: (b, h)),
            pl.BlockSpec((None, 4, HEAD_DIM), lambda b, h, i: (layer, 0, 0)),
            pl.BlockSpec((None, 1, dv), lambda b, h, i: (layer, 0, 0)),
        ],
        out_specs=pl.BlockSpec((tq, gw), lambda b, h, i: (b * nq + i, h)),
        out_shape=jax.ShapeDtypeStruct(q.shape, BF16),
        scratch_shapes=[pltpu.VMEM((2 * group, tq, 1), F32), pltpu.VMEM((2 * group, tq, 1), F32),
                        pltpu.VMEM((2 * group, tq, dv), F32)],
        compiler_params=_params(("arbitrary", "arbitrary", "arbitrary")),
    )(jnp.repeat(slopes, 2).reshape(n_heads // group, 2 * group, 1, 1), q, k, v, diff_lambda,
      subln_g.reshape(subln_g.shape[0], 1, dv))


def _diff_sample_kernel(pt_ref, slopes_ref, q_ref, kn_ref, vn_ref, *rest, pages_per_step, past, lam_init):
    pg = pages_per_step
    ck_refs = rest[:pg]
    cv_refs = rest[pg:2 * pg]
    lam_ref, g_ref, o_ref, m_ref, l_ref, acc_ref = rest[2 * pg:]
    step = pl.program_id(1)
    n_steps = pl.num_programs(1)
    ds = q_ref.shape[2]

    @pl.when(step == 0)
    def _():
        m_ref[...] = jnp.full(m_ref.shape, NEG_INF, F32)
        l_ref[...] = jnp.zeros(l_ref.shape, F32)
        acc_ref[...] = jnp.zeros(acc_ref.shape, F32)

    slope3 = slopes_ref[...]
    row = lax.broadcasted_iota(I32, (ds, PAGE_SIZE), 0)
    col = lax.broadcasted_iota(I32, (ds, PAGE_SIZE), 1)
    dist0 = row - col + (past - step * (pg * PAGE_SIZE))
    qs = (q_ref[0], q_ref[1])
    m_run = [m_ref[0], m_ref[1]]
    l_run = [l_ref[0], l_ref[1]]
    acc = [acc_ref[0], acc_ref[1]]
    for g in range(pg):
        bias = slope3 * (dist0 - g * PAGE_SIZE).astype(F32)[None]
        k3 = ck_refs[g][...].astype(BF16)
        v3 = cv_refs[g][...].astype(BF16)
        for mi in range(2):
            s = _bdot_t(qs[mi], k3[:, :, mi * HEAD_DIM:(mi + 1) * HEAD_DIM]) - bias
            m_new = jnp.maximum(m_run[mi], jnp.max(s, axis=-1, keepdims=True))
            corr = jnp.exp(m_run[mi] - m_new)
            p = jnp.exp(s - m_new)
            l_run[mi] = l_run[mi] * corr + jnp.sum(p, axis=-1, keepdims=True)
            acc[mi] = acc[mi] * corr + _bdot(p.astype(BF16), v3)
            m_run[mi] = m_new
    for mi in range(2):
        m_ref[mi], l_ref[mi], acc_ref[mi] = m_run[mi], l_run[mi], acc[mi]

    @pl.when(step == n_steps - 1)
    def _():
        r8 = lax.broadcasted_iota(I32, (ds, ds), 0)
        c8 = lax.broadcasted_iota(I32, (ds, ds), 1)
        bias8 = slope3 * (r8 - c8).astype(F32)[None]
        outs = []
        for mi in range(2):
            s = _bdot_t(qs[mi], kn_ref[mi]) - bias8
            s = jnp.where((c8 <= r8)[None], s, NEG_INF)
            m_new = jnp.maximum(m_run[mi], jnp.max(s, axis=-1, keepdims=True))
            corr = jnp.exp(m_run[mi] - m_new)
            p = jnp.exp(s - m_new)
            l_fin = l_run[mi] * corr + jnp.sum(p, axis=-1, keepdims=True)
            a_fin = acc[mi] * corr + _bdot(p.astype(BF16), vn_ref[...])
            outs.append(a_fin / l_fin)
        o_ref[...] = _diff_combine(outs[0], outs[1], lam_ref, g_ref, lam_init).astype(o_ref.dtype)


def _diff_attn_sample(q, k_new, v_new, cache_k, cache_v, page_table, slopes, diff_lambda, subln_g,
                      layer, db, ds, lam_init, *, pages_per_step=4):
    dv = 2 * HEAD_DIM
    width = q.shape[1]
    n_heads = width // dv
    n_pages = page_table.shape[1]
    pg = math.gcd(pages_per_step, n_pages)
    q5 = q.reshape(db, ds, n_heads, 2, HEAD_DIM).transpose(0, 3, 2, 1, 4)
    kn5 = k_new.reshape(db, ds, n_heads, 2, HEAD_DIM).transpose(0, 3, 2, 1, 4)
    vn4 = v_new.reshape(db, ds, n_heads, dv).transpose(0, 2, 1, 3)
    map_spec = pl.BlockSpec((None, 2, n_heads, ds, HEAD_DIM), lambda b, s, pt: (b, 0, 0, 0, 0))
    head_spec = pl.BlockSpec((None, n_heads, ds, dv), lambda b, s, pt: (b, 0, 0, 0))
    ck = cache_k.transpose(0, 1, 3, 2, 4)
    cv = cache_v.transpose(0, 1, 3, 2, 4)

    def page_spec(g):
        return pl.BlockSpec((None, None, n_heads, PAGE_SIZE, dv),
                            lambda b, s, pt: (layer, pt[b, s * pg + g], 0, 0, 0))

    kern = functools.partial(_diff_sample_kernel, pages_per_step=pg, past=n_pages * PAGE_SIZE,
                             lam_init=lam_init)
    grid_spec = pltpu.PrefetchScalarGridSpec(
        num_scalar_prefetch=1,
        grid=(db, n_pages // pg),
        in_specs=[pl.BlockSpec((n_heads, 1, 1), lambda b, s, pt: (0, 0, 0)), map_spec, map_spec, head_spec]
        + [page_spec(g) for g in range(pg)] + [page_spec(g) for g in range(pg)]
        + [pl.BlockSpec((None, 4, HEAD_DIM), lambda b, s, pt: (layer, 0, 0)),
           pl.BlockSpec((None, 1, dv), lambda b, s, pt: (layer, 0, 0))],
        out_specs=head_spec,
        scratch_shapes=[pltpu.VMEM((2, n_heads, ds, 1), F32), pltpu.VMEM((2, n_heads, ds, 1), F32),
                        pltpu.VMEM((2, n_heads, ds, dv), F32)],
    )
    out = pl.pallas_call(
        kern,
        grid_spec=grid_spec,
        out_shape=jax.ShapeDtypeStruct((db, n_heads, ds, dv), BF16),
        compiler_params=_params(("arbitrary", "arbitrary")),
    )(page_table, slopes.reshape(n_heads, 1, 1), q5, kn5, vn4, *([ck] * pg), *([cv] * pg), diff_lambda,
      subln_g.reshape(subln_g.shape[0], 1, dv))
    return out.transpose(0, 2, 1, 3).reshape(db * ds, width)


def _top_blocks(gate, n_valid, ksel):
    nb = gate.shape[-1]
    lane = lax.broadcasted_iota(I32, gate.shape, gate.ndim - 1)
    lane_f = lane.astype(F32)
    valid = lane < n_valid
    g = jnp.where(valid, gate, NEG_INF)
    picks = []
    for _ in range(ksel):
        mx = jnp.max(g, axis=-1, keepdims=True)
        idx = jnp.min(jnp.where((g == mx) & valid, lane_f, float(nb)), axis=-1, keepdims=True)
        picks.append(idx)
        hit = lane_f == idx
        g = jnp.where(hit, NEG_INF, g)
        valid = valid & jnp.logical_not(hit)
    return picks


def _block_mean_kernel(k_ref, o_ref):
    o_ref[...] = jnp.mean(k_ref[...], axis=0, keepdims=True)


def _block_means(k32, rows):
    t, width = k32.shape
    out = pl.pallas_call(
        _block_mean_kernel,
        grid=(t // rows,),
        in_specs=[pl.BlockSpec((rows, width), lambda i: (i, 0))],
        out_specs=pl.BlockSpec((None, 1, width), lambda i: (i, 0, 0)),
        out_shape=jax.ShapeDtypeStruct((t // rows, 1, width), F32),
        compiler_params=_params(("arbitrary",)),
    )(k32)
    return out.reshape(t // rows, width)


def _heads(ref, rows, n, width):
    return jnp.stack([ref[rows, i * width:(i + 1) * width] for i in range(n)], axis=0)


def _moba_prompt_kernel(slopes_ref, q_ref, k_ref, v_ref, km_ref, o_ref, m_ref, l_ref, acc_ref, *, ksel,
                        group):
    qi = pl.program_id(2)
    blk = MOBA_BLOCK
    slope3 = slopes_ref[...]
    q3 = _heads(q_ref, slice(None), group, HEAD_DIM)
    row = lax.broadcasted_iota(I32, (blk, blk), 0)
    col = lax.broadcasted_iota(I32, (blk, blk), 1)
    rel = slope3 * (col - row).astype(F32)[None]

    d0 = pl.ds(pl.multiple_of(qi * blk, blk), blk)
    s = _bdot_t(q3, _heads(k_ref, d0, group, HEAD_DIM)) + rel
    s = jnp.where((col <= row)[None], s, NEG_INF)
    m0 = jnp.max(s, axis=-1, keepdims=True)
    p = jnp.exp(s - m0)
    m_ref[...] = m0
    l_ref[...] = jnp.sum(p, axis=-1, keepdims=True)
    acc_ref[...] = _bdot(p.astype(BF16), _heads(v_ref, d0, group, HEAD_DIM))

    gate = _bdot_t(q3, _heads(km_ref, slice(None), group, HEAD_DIM).astype(BF16))
    picks = _top_blocks(gate, qi, ksel)

    def body(kj, c):
        k0 = pl.ds(pl.multiple_of(kj * blk, blk), blk)
        kj_f = kj.astype(F32)
        chosen = picks[0] == kj_f
        for idx in picks[1:]:
            chosen = chosen | (idx == kj_f)
        row_bias = jnp.where(chosen, 0.0, NEG_INF) - slope3 * ((qi - kj) * blk).astype(F32)
        s = _bdot_t(q3, _heads(k_ref, k0, group, HEAD_DIM)) + rel + row_bias
        m_prev = m_ref[...]
        m_new = jnp.maximum(m_prev, jnp.max(s, axis=-1, keepdims=True))
        corr = jnp.exp(m_prev - m_new)
        p = jnp.exp(s - m_new)
        l_ref[...] = corr * l_ref[...] + jnp.sum(p, axis=-1, keepdims=True)
        acc_ref[...] = corr * acc_ref[...] + _bdot(p.astype(BF16), _heads(v_ref, k0, group, HEAD_DIM))
        m_ref[...] = m_new
        return c

    lax.fori_loop(0, qi, body, 0)
    out = acc_ref[...] / l_ref[...]
    for i in range(group):
        o_ref[:, i * HEAD_DIM:(i + 1) * HEAD_DIM] = out[i].astype(o_ref.dtype)


def _moba_prompt(q16, k16, v16, kmean, slopes, bsz, seq, *, group=8):
    width = q16.shape[1]
    n_heads = width // HEAD_DIM
    assert seq % MOBA_BLOCK == 0 and n_heads % group == 0
    nb = seq // MOBA_BLOCK
    gw = group * HEAD_DIM
    ksel = min(MOBA_TOPK, nb - 1)
    if ksel == 0:
        ksel = 1
    kern = functools.partial(_moba_prompt_kernel, ksel=ksel, group=group)
    return pl.pallas_call(
        kern,
        grid=(bsz, n_heads // group, nb),
        in_specs=[
            pl.BlockSpec((None, group, 1, 1), lambda b, h, i: (h, 0, 0, 0)),
            pl.BlockSpec((MOBA_BLOCK, gw), lambda b, h, i: (b * nb + i, h)),
            pl.BlockSpec((seq, gw), lambda b, h, i: (b, h)),
            pl.BlockSpec((seq, gw), lambda b, h, i: (b, h)),
            pl.BlockSpec((None, nb, gw), lambda b, h, i: (b, 0, h)),
        ],
        out_specs=pl.BlockSpec((MOBA_BLOCK, gw), lambda b, h, i: (b * nb + i, h)),
        out_shape=jax.ShapeDtypeStruct(q16.shape, BF16),
        scratch_shapes=[pltpu.VMEM((group, MOBA_BLOCK, 1), F32), pltpu.VMEM((group, MOBA_BLOCK, 1), F32),
                        pltpu.VMEM((group, MOBA_BLOCK, HEAD_DIM), F32)],
        compiler_params=_params(("arbitrary", "arbitrary", "arbitrary")),
    )(slopes.reshape(n_heads // group, group, 1, 1), q16, k16, v16, kmean)


def _page_block_mean_kernel(pt_ref, *refs, bp):
    o_ref = refs[bp]
    tot = None
    for g in range(bp):
        part = jnp.sum(refs[g][...], axis=0)
        tot = part if tot is None else tot + part
    o_ref[...] = tot * (1.0 / (bp * PAGE_SIZE))


def _paged_block_means(cache_k, page_table, layer, n_full):
    db = page_table.shape[0]
    bp = MOBA_BLOCK // PAGE_SIZE
    n_heads, dh = cache_k.shape[3], cache_k.shape[4]

    def page_spec(g):
        return pl.BlockSpec((None, None, PAGE_SIZE, n_heads, dh),
                            lambda b, n, pt: (layer, pt[b, n * bp + g], 0, 0, 0))

    grid_spec = pltpu.PrefetchScalarGridSpec(
        num_scalar_prefetch=1,
        grid=(db, n_full),
        in_specs=[page_spec(g) for g in range(bp)],
        out_specs=pl.BlockSpec((None, None, n_heads, dh), lambda b, n, pt: (b, n, 0, 0)),
    )
    return pl.pallas_call(
        functools.partial(_page_block_mean_kernel, bp=bp),
        grid_spec=grid_spec,
        out_shape=jax.ShapeDtypeStruct((db, n_full, n_heads, dh), F32),
        compiler_params=_params(("arbitrary", "arbitrary")),
    )(page_table, *([cache_k] * bp))


def _moba_select_kernel(q_ref, km_ref, o_ref, *, n_heads, ksel):
    n_full = km_ref.shape[1]
    lane = lax.broadcasted_iota(I32, (q_ref.shape[0], 128), 1)
    for h in range(n_heads):
        c0 = h * HEAD_DIM
        gate = _dot_t(q_ref[:, c0:c0 + HEAD_DIM], km_ref[h].astype(BF16))
        picks = _top_blocks(gate, n_full, ksel)
        out = jnp.zeros(lane.shape, I32)
        for j, idx in enumerate(picks):
            out = jnp.where(lane == j, idx.astype(I32), out)
        o_ref[h] = out


def _moba_select(q16, kmean, db, ds, ksel):
    width = q16.shape[1]
    n_heads = width // HEAD_DIM
    n_full = kmean.shape[2]
    out = pl.pallas_call(
        functools.partial(_moba_select_kernel, n_heads=n_heads, ksel=ksel),
        grid=(db,),
        in_specs=[pl.BlockSpec((ds, width), lambda b: (b, 0)),
                  pl.BlockSpec((None, n_heads, n_full, HEAD_DIM), lambda b: (b, 0, 0, 0))],
        out_specs=pl.BlockSpec((None, n_heads, ds, 128), lambda b: (b, 0, 0, 0)),
        out_shape=jax.ShapeDtypeStruct((db, n_heads, ds, 128), I32),
        compiler_params=_params(("arbitrary",)),
    )(q16, kmean)
    return out[..., :ksel]


def _moba_sample_kernel(pt_ref, sel_ref, slopes_ref, q_ref, kn_ref, vn_ref, ck_hbm, cv_hbm, o_ref,
                        kbuf, vbuf, ksem, vsem, *, layer, ksel, bp, past):
    b = pl.program_id(0)
    h = pl.program_id(1)
    n_heads = pl.num_programs(1)
    t = b * n_heads + h
    slot = lax.rem(t, 2)
    ds = q_ref.shape[0]
    n_tiles = ksel * bp

    def copies(bb, hh, sl):
        out = []
        base = (bb * n_heads + hh) * ds * ksel
        for qn in range(ds):
            for j in range(ksel):
                blk = sel_ref[base + qn * ksel + j]
                for g in range(bp):
                    page = pt_ref[bb, blk * bp + g]
                    out.append(pltpu.make_async_copy(ck_hbm.at[layer, page, :, hh, :],
                                                     kbuf.at[sl, qn, j * bp + g], ksem.at[sl]))
                    out.append(pltpu.make_async_copy(cv_hbm.at[layer, page, :, hh, :],
                                                     vbuf.at[sl, qn, j * bp + g], vsem.at[sl]))
        return out

    @pl.when(t == 0)
    def _():
        for c in copies(b, h, slot):
            c.start()

    @pl.when(t + 1 < pl.num_programs(0) * n_heads)
    def _():
        wrap = h + 1 == n_heads
        for c in copies(jnp.where(wrap, b + 1, b), jnp.where(wrap, 0, h + 1), 1 - slot):
            c.start()

    for c in copies(b, h, slot):
        c.wait()

    slope = slopes_ref[h]
    q16 = q_ref[...]
    k3 = kbuf[slot].reshape(ds, n_tiles * PAGE_SIZE, HEAD_DIM).astype(BF16)
    v3 = vbuf[slot].reshape(ds, n_tiles * PAGE_SIZE, HEAD_DIM).astype(BF16)
    q3 = jnp.broadcast_to(q16[None], (ds, ds, HEAD_DIM))
    row = lax.broadcasted_iota(I32, (ds, PAGE_SIZE), 0)
    col = lax.broadcasted_iota(I32, (ds, PAGE_SIZE), 1)
    base = t * (ds * ksel)
    dist_rows = []
    for qn in range(ds):
        pieces = []
        for j in range(ksel):
            blk_start = sel_ref[base + qn * ksel + j] * MOBA_BLOCK
            for g in range(bp):
                pieces.append((row - col + (past - blk_start - g * PAGE_SIZE)).astype(F32))
        dist_rows.append(jnp.concatenate(pieces, axis=-1))
    dist3 = jnp.stack(dist_rows, axis=0)
    s = _bdot_t(q3, k3) - slope * dist3
    r8 = lax.broadcasted_iota(I32, (ds, ds), 0)
    c8 = lax.broadcasted_iota(I32, (ds, ds), 1)
    s_own = _dot_t(q16, kn_ref[...]) - slope * (r8 - c8).astype(F32)
    s_own = jnp.where(c8 <= r8, s_own, NEG_INF)
    s_own3 = jnp.broadcast_to(s_own[None], (ds, ds, ds))
    mx = jnp.maximum(jnp.max(s, axis=-1, keepdims=True), jnp.max(s_own3, axis=-1, keepdims=True))
    p = jnp.exp(s - mx)
    p_own = jnp.exp(s_own3 - mx)
    den = jnp.sum(p, axis=-1, keepdims=True) + jnp.sum(p_own, axis=-1, keepdims=True)
    out = _bdot((p / den).astype(BF16), v3)
    own = jnp.dot((p_own / den).reshape(ds * ds, ds).astype(BF16), vn_ref[...], preferred_element_type=F32)
    out = out + own.reshape(ds, ds, HEAD_DIM)
    keep = lax.broadcasted_iota(I32, (ds, HEAD_DIM), 0)
    res = jnp.zeros((ds, HEAD_DIM), F32)
    for qn in range(ds):
        res = res + jnp.where(keep == qn, out[qn], 0.0)
    o_ref[...] = res.astype(o_ref.dtype)


def _moba_sample(q16, k_new, v_new, cache_k, cache_v, page_table, sel, slopes, layer, db, ds):
    width = q16.shape[1]
    n_heads = width // HEAD_DIM
    ksel = sel.shape[-1]
    bp = MOBA_BLOCK // PAGE_SIZE
    n_pages = page_table.shape[1]
    n_tiles = ksel * bp
    head_spec = pl.BlockSpec((ds, HEAD_DIM), lambda b, h, pt, sl: (b, h))
    hbm = pl.BlockSpec(memory_space=pl.ANY)
    tile_buf = pltpu.VMEM((2, ds, n_tiles, PAGE_SIZE, HEAD_DIM), F32)
    grid_spec = pltpu.PrefetchScalarGridSpec(
        num_scalar_prefetch=2,
        grid=(db, n_heads),
        in_specs=[pl.BlockSpec(memory_space=pltpu.SMEM), head_spec, head_spec, head_spec, hbm, hbm],
        out_specs=head_spec,
        scratch_shapes=[tile_buf, tile_buf, pltpu.SemaphoreType.DMA((2,)), pltpu.SemaphoreType.DMA((2,))],
    )
    kern = functools.partial(_moba_sample_kernel, layer=layer, ksel=ksel, bp=bp, past=n_pages * PAGE_SIZE)
    return pl.pallas_call(
        kern,
        grid_spec=grid_spec,
        out_shape=jax.ShapeDtypeStruct(q16.shape, BF16),
        compiler_params=_params(("arbitrary", "arbitrary")),
    )(page_table, sel.reshape(-1), slopes, q16, k_new, v_new, cache_k, cache_v)


def _layer_norm_rows(z, g_ref, b_ref):
    mu = jnp.mean(z, axis=-1, keepdims=True)
    zc = z - mu
    var = jnp.mean(zc * zc, axis=-1, keepdims=True)
    return zc * lax.rsqrt(var + LN_EPS) * g_ref[...] + b_ref[...]


def _route(logits):
    lane = lax.broadcasted_iota(I32, logits.shape, 1)
    lane_f = lane.astype(F32)
    none = float(ROUTER_LANES)
    is_g = lane < MOE_GROUPS
    lg = jnp.where(is_g, logits, NEG_INF)
    mg = jnp.max(lg, axis=-1, keepdims=True)
    g_top = jnp.min(jnp.where(lg == mg, lane_f, none), axis=-1, keepdims=True)
    gate_g = 1.0 / jnp.sum(jnp.where(is_g, jnp.exp(logits - mg), 0.0), axis=-1, keepdims=True)
    e_lo = MOE_GROUPS + g_top * MOE_PER_GROUP
    le = jnp.where((lane_f >= e_lo) & (lane_f < e_lo + MOE_PER_GROUP), logits, NEG_INF)
    v1 = jnp.max(le, axis=-1, keepdims=True)
    i1 = jnp.min(jnp.where(le == v1, lane_f, none), axis=-1, keepdims=True)
    le2 = jnp.where(lane_f == i1, NEG_INF, le)
    v2 = jnp.max(le2, axis=-1, keepdims=True)
    i2 = jnp.min(jnp.where(le2 == v2, lane_f, none), axis=-1, keepdims=True)
    e2 = jnp.exp(v2 - v1)
    w1 = gate_g / (1.0 + e2)
    w2 = gate_g * e2 / (1.0 + e2)
    eid = jnp.where(lane == 0, i1 - MOE_GROUPS, jnp.where(lane == 1, i2 - MOE_GROUPS, 0.0)).astype(I32)
    wgt = jnp.where(lane == 0, w1, jnp.where(lane == 1, w2, 0.0))
    return eid, wgt


def _ln_router_kernel(z_ref, g_ref, b_ref, rw_ref, rb_ref, y32_ref, eid_ref, wgt_ref):
    y = _layer_norm_rows(z_ref[...], g_ref, b_ref)
    y32_ref[...] = y
    logits = jnp.dot(y.astype(BF16), rw_ref[...], preferred_element_type=F32) + rb_ref[...]
    eid, wgt = _route(logits)
    eid_ref[...] = eid
    wgt_ref[...] = wgt


def _ln_router(z, ln_g, ln_b, layer, which, rw, rb, *, tm=256):
    t, d = z.shape
    tm = min(tm, t)
    assert t % tm == 0
    row = pl.BlockSpec((tm, d), lambda i: (i, 0))
    vec = pl.BlockSpec((None, None, 1, d), lambda i: (layer, which, 0, 0))
    lane_blk = pl.BlockSpec((tm, ROUTER_LANES), lambda i: (i, 0))
    return pl.pallas_call(
        _ln_router_kernel,
        grid=(t // tm,),
        in_specs=[row, vec, vec, pl.BlockSpec((d, ROUTER_LANES), lambda i: (0, 0)),
                  pl.BlockSpec((1, ROUTER_LANES), lambda i: (0, 0))],
        out_specs=[row, lane_blk, lane_blk],
        out_shape=[jax.ShapeDtypeStruct((t, d), F32),
                   jax.ShapeDtypeStruct((t, ROUTER_LANES), I32),
                   jax.ShapeDtypeStruct((t, ROUTER_LANES), F32)],
        compiler_params=_params(("arbitrary",)),
    )(z, ln_g.reshape(ln_g.shape[0], ln_g.shape[1], 1, d), ln_b.reshape(ln_b.shape[0], ln_b.shape[1], 1, d),
      rw, rb)


def _ln_combine_kernel(x_ref, y0_ref, y1_ref, w_ref, g_ref, b_ref, y32_ref, y16_ref, *, alpha):
    w = w_ref[...]
    z = alpha * x_ref[...] + (w[:, 0:1] * y0_ref[...] + w[:, 1:2] * y1_ref[...])
    y = _layer_norm_rows(z, g_ref, b_ref)
    y32_ref[...] = y
    y16_ref[...] = y.astype(BF16)


def _ln_combine(x, y, wgt, ln_g, ln_b, layer, which, alpha, *, tm=256):
    t, d = x.shape
    tm = min(tm, t)
    assert t % tm == 0
    row = pl.BlockSpec((tm, d), lambda i: (i, 0))
    vec = pl.BlockSpec((None, None, 1, d), lambda i: (layer, which, 0, 0))
    return pl.pallas_call(
        functools.partial(_ln_combine_kernel, alpha=alpha),
        grid=(t // tm,),
        in_specs=[row, pl.BlockSpec((None, tm, d), lambda i: (0, i, 0)),
                  pl.BlockSpec((None, tm, d), lambda i: (1, i, 0)),
                  pl.BlockSpec((tm, ROUTER_LANES), lambda i: (i, 0)), vec, vec],
        out_specs=[row, row],
        out_shape=[jax.ShapeDtypeStruct((t, d), F32), jax.ShapeDtypeStruct((t, d), BF16)],
        compiler_params=_params(("arbitrary",)),
    )(x, y, y, wgt, ln_g.reshape(ln_g.shape[0], ln_g.shape[1], 1, d),
      ln_b.reshape(ln_b.shape[0], ln_b.shape[1], 1, d))


def _moe_kernel(te_ref, nv_ref, tab_ref, x_hbm, w1_ref, w3_ref, w2_ref, out_hbm, xbuf, obuf, gsem, ssem, *,
                n_asg):
    t = pl.program_id(0)
    c = pl.program_id(1)
    n_tiles = pl.num_programs(0)
    n_chunks = pl.num_programs(1)
    nv = nv_ref[0]
    sl = lax.rem(t, 2)
    n_tok = n_asg // MOE_TOPK

    def slot_asg(tile, r):
        idx = tile * MOE_TILE + r
        word = tab_ref[lax.shift_right_logical(idx, 1)]
        return jnp.where((idx & 1) == 0, word & 0xFFFF, lax.shift_right_logical(word, 16))

    def gather(tile, buf, start):
        def body(r, carry):
            tok = lax.shift_right_logical(jnp.minimum(slot_asg(tile, r), n_asg - 1), 1)
            cp = pltpu.make_async_copy(x_hbm.at[pl.ds(tok, 1)], xbuf.at[buf, pl.ds(r, 1)], gsem.at[buf])
            if start:
                cp.start()
            else:
                cp.wait()
            return carry
        lax.fori_loop(0, MOE_TILE, body, 0, unroll=8)

    def scatter(tile, buf, start):
        def body(r, carry):
            a = slot_asg(tile, r)

            @pl.when(a < n_asg)
            def _():
                row = (a & 1) * n_tok + lax.shift_right_logical(a, 1)
                cp = pltpu.make_async_copy(obuf.at[buf, pl.ds(r, 1)], out_hbm.at[pl.ds(row, 1)], ssem.at[buf])
                if start:
                    cp.start()
                else:
                    cp.wait()
            return carry
        lax.fori_loop(0, MOE_TILE, body, 0, unroll=8)

    @pl.when(c == 0)
    def _():
        @pl.when(t == 0)
        def _():
            gather(t, sl, True)

        @pl.when(t < nv)
        def _():
            gather(t, sl, False)

        @pl.when(t + 1 < nv)
        def _():
            gather(t + 1, 1 - sl, True)

        @pl.when((t >= 2) & (t - 2 < nv))
        def _():
            scatter(t - 2, sl, False)

    @pl.when(t < nv)
    def _():
        x = xbuf[sl].astype(BF16)
        a = jnp.dot(x, w1_ref[...].astype(BF16), preferred_element_type=F32)
        b = jnp.dot(x, w3_ref[...].astype(BF16), preferred_element_type=F32)
        hid = (a * jax.nn.sigmoid(a)) * b
        y = jnp.dot(hid.astype(BF16), w2_ref[...].astype(BF16), preferred_element_type=F32)

        @pl.when(c == 0)
        def _():
            obuf[sl] = y

        @pl.when(c != 0)
        def _():
            obuf[sl] += y

        @pl.when(c == n_chunks - 1)
        def _():
            scatter(t, sl, True)

    @pl.when((t == n_tiles - 1) & (c == n_chunks - 1))
    def _():
        @pl.when((t >= 1) & (t - 1 < nv))
        def _():
            scatter(t - 1, 1 - sl, False)

        @pl.when(t < nv)
        def _():
            scatter(t, sl, False)


def _moe_experts(x32, slot_table, tile_expert, n_valid, w1, w3, w2, layer, n_tiles, *, n_chunks=2):
    n_tok, d = x32.shape
    n_asg = n_tok * MOE_TOPK
    d_exp = w1.shape[-1]
    ce = d_exp // n_chunks

    def chunk(t, c, nv):
        return jnp.where(t >= nv[0], 0, jnp.where(t % 2 == 0, c, n_chunks - 1 - c))

    grid_spec = pltpu.PrefetchScalarGridSpec(
        num_scalar_prefetch=3,
        grid=(n_tiles, n_chunks),
        in_specs=[
            pl.BlockSpec(memory_space=pl.ANY),
            pl.BlockSpec((None, None, d, ce), lambda t, c, te, nv, tab: (layer, te[t], 0, chunk(t, c, nv))),
            pl.BlockSpec((None, None, d, ce), lambda t, c, te, nv, tab: (layer, te[t], 0, chunk(t, c, nv))),
            pl.BlockSpec((None, None, ce, d), lambda t, c, te, nv, tab: (layer, te[t], chunk(t, c, nv), 0)),
        ],
        out_specs=pl.BlockSpec(memory_space=pl.ANY),
        scratch_shapes=[pltpu.VMEM((2, MOE_TILE, d), F32), pltpu.VMEM((2, MOE_TILE, d), F32),
                        pltpu.SemaphoreType.DMA((2,)), pltpu.SemaphoreType.DMA((2,))],
    )
    return pl.pallas_call(
        functools.partial(_moe_kernel, n_asg=n_asg),
        grid_spec=grid_spec,
        out_shape=jax.ShapeDtypeStruct((n_asg, d), F32),
        compiler_params=_params(("arbitrary", "arbitrary")),
    )(tile_expert, n_valid, slot_table, x32, w1, w3, w2)


def _moe_layer(x32, eid, wgt, ln_g, ln_b, layer, w1, w3, w2, alpha):
    n_tok, d = x32.shape
    n_asg = n_tok * MOE_TOPK
    assert n_asg < (1 << 16) and MOE_TOPK == 2
    e_flat = eid[:, :MOE_TOPK].reshape(n_asg)
    onehot = (e_flat[:, None] == jnp.arange(N_EXPERTS, dtype=I32)[None, :]).astype(I32)
    csum = jnp.cumsum(onehot, axis=0)
    rank = jnp.sum(csum * onehot, axis=1) - 1
    counts = csum[-1]
    padded = (counts + MOE_TILE - 1) // MOE_TILE * MOE_TILE
    pad_ends = jnp.cumsum(padded)
    dest = (pad_ends - padded)[e_flat] + rank
    n_tiles = -(-(n_asg + N_EXPERTS * (MOE_TILE - 1)) // MOE_TILE)
    slot_asg = jnp.full((n_tiles * MOE_TILE,), n_asg, I32).at[dest].set(jnp.arange(n_asg, dtype=I32))
    slot_table = slot_asg[0::2] | (slot_asg[1::2] << 16)
    tile_start = jnp.arange(n_tiles, dtype=I32) * MOE_TILE
    n_valid = (pad_ends[-1] // MOE_TILE).astype(I32)
    tile_expert = jnp.searchsorted(pad_ends, jnp.minimum(tile_start, pad_ends[-1] - 1), side="right")
    tile_expert = jnp.minimum(tile_expert, N_EXPERTS - 1).astype(I32)
    y = _moe_experts(x32, slot_table, tile_expert, n_valid.reshape(1), w1, w3, w2, layer, n_tiles)
    y = y.reshape(MOE_TOPK, n_tok, d)
    return _ln_combine(x32, y, wgt, ln_g, ln_b, layer, 1, alpha, tm=_row_tile(n_tok, 256))


def _router_weights(router_g_w, router_g_b, router_e_w, router_e_b, layer):
    d = router_g_w.shape[1]
    pad = ROUTER_LANES - MOE_GROUPS - N_EXPERTS
    rw = jnp.concatenate([router_g_w[layer], router_e_w[layer], jnp.zeros((d, pad), F32)], axis=1).astype(BF16)
    rb = jnp.concatenate([router_g_b[layer], router_e_b[layer], jnp.zeros((pad,), F32)])[None, :]
    return rw, rb


def kernel(x_prompt, x_sample, cache_k_diff, cache_v_diff, state_pool, cache_k_moba, cache_v_moba, page_table, w_in_even, pool_w, pool_scale, diff_lambda, diff_subln_g, w_out_even, w_in_odd, w_out_odd, ln_g, ln_b, router_g_w, router_g_b, router_e_w, router_e_b, moe_w1, moe_w3, moe_w2):
    bsz, seq, d = x_prompt.shape
    db, ds, _ = x_sample.shape
    depth = ln_g.shape[0]
    alpha = (2.0 * depth) ** 0.25
    n_pages = page_table.shape[1]
    past = n_pages * PAGE_SIZE
    tp, ts = bsz * seq, db * ds
    pool_width = pool_scale.shape[1]
    diff_width = cache_k_diff.shape[3] * cache_k_diff.shape[4]
    moba_width = cache_k_moba.shape[3] * cache_k_moba.shape[4]
    slopes_diff = jnp.asarray(_alibi_slopes(cache_k_diff.shape[3]))
    slopes_moba = jnp.asarray(_alibi_slopes(cache_k_moba.shape[3]))
    scale = HEAD_DIM ** -0.5
    tm_p = 512

    xp32, xs32 = x_prompt.reshape(tp, d), x_sample.reshape(ts, d)
    assert tp % ts == 0
    groups = ((xp32.astype(BF16), xp32, 0, tp, tm_p, True), (xs32.astype(BF16), xs32, 0, ts, ts, False))
    kd_p, vd_p, pl_p, km_p, vm_p = [], [], [], [], []
    kd_s, vd_s, pl_s, km_s, vm_s = [], [], [], [], []
    for i in range(depth):
        j = i // 2
        if i % 2 == 0:
            lam_init = 0.8 - 0.6 * math.exp(-0.3 * i)
            zs = []
            for x16, x32, row0, m, tm, is_prompt in groups:
                (u32,) = _matmul([x16], w_in_even, j, 0, pool_width, [(F32, 1.0)], tm=tm, m=m,
                                 x_row0=row0)
                (q16,) = _matmul([x16], w_in_even, j, pool_width, diff_width, [(BF16, scale)], tm=tm, m=m,
                                 x_row0=row0)
                k32, k16 = _matmul([x16], w_in_even, j, pool_width + diff_width, diff_width,
                                   [(F32, 1.0), (BF16, 1.0)], tm=tm, m=m, x_row0=row0)
                v32, v16 = _matmul([x16], w_in_even, j, pool_width + 2 * diff_width, diff_width,
                                   [(F32, 1.0), (BF16, 1.0)], tm=tm, m=m, x_row0=row0)
                if is_prompt:
                    u3 = u32.reshape(bsz, seq, pool_width)
                    prefix = jnp.zeros((bsz, POOL_HALO, pool_width), F32)
                    pool = _pool_mix(u3, prefix, pool_w, pool_scale, j, 0, tl=256)
                    att = _diff_attn_prompt(q16, k16, v16, slopes_diff, diff_lambda, diff_subln_g, j,
                                            bsz, seq, lam_init)
                    ext = u3 if seq >= POOL_CARRY else jnp.concatenate([prefix[:, :POOL_CARRY], u3], axis=1)
                    kd_p.append(k32.reshape(bsz, seq, -1, 2 * HEAD_DIM))
                    vd_p.append(v32.reshape(bsz, seq, -1, 2 * HEAD_DIM))
                    pl_p.append(ext[:, -POOL_CARRY:])
                else:
                    u3 = u32.reshape(db, ds, pool_width)
                    prefix = jnp.concatenate([jnp.zeros((db, POOL_HALO - POOL_CARRY, pool_width), F32),
                                              state_pool[j]], axis=1)
                    pool = _pool_mix(u3, prefix, pool_w, pool_scale, j, past, tl=ds)
                    att = _diff_attn_sample(q16, k16, v16, cache_k_diff, cache_v_diff, page_table,
                                            slopes_diff, diff_lambda, diff_subln_g, j, db, ds, lam_init)
                    ext = jnp.concatenate([state_pool[j], u3], axis=1)
                    kd_s.append(k32.reshape(db, ds, -1, 2 * HEAD_DIM))
                    vd_s.append(v32.reshape(db, ds, -1, 2 * HEAD_DIM))
                    pl_s.append(ext[:, -POOL_CARRY:])
                (z,) = _matmul([pool, att], w_out_even, j, 0, d, [(F32, 1.0)], tm=tm, resid=x32, alpha=alpha,
                               resid_row0=row0)
                zs.append(z)
        else:
            zs = []
            for x16, x32, row0, m, tm, is_prompt in groups:
                (q16,) = _matmul([x16], w_in_odd, j, 0, moba_width, [(BF16, scale)], tm=tm, m=m, x_row0=row0)
                k32, k16 = _matmul([x16], w_in_odd, j, moba_width, moba_width, [(F32, 1.0), (BF16, 1.0)], tm=tm,
                                   m=m, x_row0=row0)
                v32, v16 = _matmul([x16], w_in_odd, j, 2 * moba_width, moba_width, [(F32, 1.0), (BF16, 1.0)],
                                   tm=tm, m=m, x_row0=row0)
                if is_prompt:
                    kmean = _block_means(k32, MOBA_BLOCK).reshape(bsz, seq // MOBA_BLOCK, moba_width)
                    att = _moba_prompt(q16, k16, v16, kmean, slopes_moba, bsz, seq)
                    km_p.append(k32.reshape(bsz, seq, -1, HEAD_DIM))
                    vm_p.append(v32.reshape(bsz, seq, -1, HEAD_DIM))
                else:
                    n_full = past // MOBA_BLOCK
                    assert past == n_full * MOBA_BLOCK and n_full >= 1
                    ksel = min(MOBA_TOPK, n_full)
                    kmean = _paged_block_means(cache_k_moba, page_table, j, n_full)
                    sel = _moba_select(q16, kmean.transpose(0, 2, 1, 3), db, ds, ksel)
                    att = _moba_sample(q16, k16, v16, cache_k_moba, cache_v_moba, page_table,
                                       sel, slopes_moba, j, db, ds)
                    km_s.append(k32.reshape(db, ds, -1, HEAD_DIM))
                    vm_s.append(v32.reshape(db, ds, -1, HEAD_DIM))
                (z,) = _matmul([att], w_out_odd, j, 0, d, [(F32, 1.0)], tm=tm, resid=x32, alpha=alpha,
                               resid_row0=row0)
                zs.append(z)
        rw, rb = _router_weights(router_g_w, router_g_b, router_e_w, router_e_b, i)
        outs = [_ln_router(z, ln_g, ln_b, i, 0, rw, rb) for z in zs]
        x32 = jnp.concatenate([o[0] for o in outs], axis=0)
        eid = jnp.concatenate([o[1] for o in outs], axis=0)
        wgt = jnp.concatenate([o[2] for o in outs], axis=0)
        y32, y16 = _moe_layer(x32, eid, wgt, ln_g, ln_b, i, moe_w1, moe_w3, moe_w2, alpha)
        groups = ((y16, y32, 0, tp, tm_p, True), (y16, y32, tp, ts, ts, False))
    return (y32[:tp].reshape(bsz, seq, d), y32[tp:].reshape(db, ds, d),
            jnp.stack(kd_p), jnp.stack(vd_p), jnp.stack(pl_p), jnp.stack(km_p), jnp.stack(vm_p),
            jnp.stack(kd_s), jnp.stack(vd_s), jnp.stack(pl_s), jnp.stack(km_s), jnp.stack(vm_s))
```
